```python
import math, functools
import jax, jax.numpy as jnp
from jax import lax
import numpy as np

D_MODEL = 2048
BATCH = 1
SEQ = 16384
DEPTH = 4
DEC_BATCH = 16
DEC_SEQ = 32
PAST_LEN = 4096

CHUNK = 64
D_MIX = D_MODEL // 2
W_GROUP = D_MIX // 4
GDN_HEADS = 4
GDN_HD = W_GROUP // GDN_HEADS
CONV_W = 4
ML_HEADS = 4
ML_HD = W_GROUP // ML_HEADS
S5_CH = 16
S5_GROUPS = W_GROUP // S5_CH
S5_P = 64
SB_HEADS = 4
SB_HD = W_GROUP // SB_HEADS
SB_BLOCK = 128
SB_KB = 64
SB_SEGMENTS = 16
D_FF = 2048
EPS = 1e-6
GDN_IN = 3 * W_GROUP + 2 * GDN_HEADS + W_GROUP
ML_IN = 3 * W_GROUP + 2 * ML_HEADS + W_GROUP
S5_IN = W_GROUP
SB_IN = 3 * W_GROUP
N_IN = GDN_IN + ML_IN + S5_IN + SB_IN

kernel_name = "hybrid_streaming_encoder_step"


def rms_norm(x, g):
    xf = x.astype(jnp.float32)
    y = xf * lax.rsqrt(jnp.mean(xf * xf, axis=-1, keepdims=True) + EPS)
    return (y * g.astype(jnp.float32)).astype(x.dtype)


def l2_normalize(x):
    return x * lax.rsqrt(jnp.sum(x * x, axis=-1, keepdims=True) + EPS)


def swiglu_ffn(x, w_gate, w_up, w_down):
    return (jax.nn.silu(x @ w_gate) * (x @ w_up)) @ w_down


def causal_conv(x, buf, w):
    L = x.shape[1]
    xp = jnp.concatenate([buf.astype(x.dtype), x], axis=1)
    y = sum(xp[:, i:i + L] * w[i] for i in range(CONV_W))
    return y, xp[:, L:]


def to_chunks(x, c):
    B, L = x.shape[:2]
    x = x.reshape((B, L // c, c) + x.shape[2:])
    if x.ndim == 5:
        return x.transpose(1, 0, 3, 2, 4)
    return x.transpose(1, 0, 3, 2)


def from_chunks(y):
    n, B, H, c, d = y.shape
    return y.transpose(1, 0, 3, 2, 4).reshape(B, n * c, H, d)


def gated_delta_chunked(q, k, v, log_a, beta, s0):
    c = min(CHUNK, q.shape[1])
    dv = v.shape[-1]
    qc, kc, vc = to_chunks(q, c), to_chunks(k, c), to_chunks(v, c)
    gc = jnp.cumsum(to_chunks(log_a, c), axis=-1)
    bc = to_chunks(beta, c)
    idx = jnp.arange(c)
    incl = idx[:, None] >= idx[None, :]
    strict = idx[:, None] > idx[None, :]
    decay = jnp.exp(jnp.where(incl, gc[..., :, None] - gc[..., None, :], -jnp.inf))
    kk = jnp.einsum('nbhid,nbhjd->nbhij', kc, kc)
    m = jnp.where(strict, bc[..., :, None] * kk * decay, 0.0)
    gamma = jnp.exp(gc)
    rhs = jnp.concatenate([bc[..., None] * vc, (bc * gamma)[..., None] * kc], axis=-1)
    w = lax.linalg.triangular_solve(jnp.eye(c, dtype=m.dtype) + m, rhs,
                                    left_side=True, lower=True, unit_diagonal=True)
    w_v, w_k = w[..., :dv], w[..., dv:]
    p = jnp.einsum('nbhid,nbhjd->nbhij', qc, kc) * decay
    qg = qc * gamma[..., None]
    kg = kc * jnp.exp(gc[..., -1:] - gc)[..., None]
    g_last = jnp.exp(gc[..., -1])

    def step(s, xs):
        wv, wk, pp, qq, kq, gl = xs
        u = wv - jnp.einsum('bhik,bhkv->bhiv', wk, s)
        o = jnp.einsum('bhik,bhkv->bhiv', qq, s) + jnp.einsum('bhij,bhjv->bhiv', pp, u)
        s = gl[..., None, None] * s + jnp.einsum('bhjk,bhjv->bhkv', kq, u)
        return s, o

    s_fin, o = lax.scan(step, s0, (w_v, w_k, p, qg, kg, g_last))
    return from_chunks(o), s_fin


def mlstm_chunked(q, k, v, i_pre, log_f, c0, n0, m0):
    c = min(CHUNK, q.shape[1])
    qc, kc, vc = to_chunks(q, c), to_chunks(k, c), to_chunks(v, c)
    ic = to_chunks(i_pre, c)
    bcum = jnp.cumsum(to_chunks(log_f, c), axis=-1)
    idx = jnp.arange(c)
    incl = idx[:, None] >= idx[None, :]
    dmat = jnp.where(incl, bcum[..., :, None] - bcum[..., None, :] + ic[..., None, :], -jnp.inf)
    dmax = jnp.max(dmat, axis=-1)
    qk = jnp.einsum('nbhid,nbhjd->nbhij', qc, kc)
    dlast = bcum[..., -1:] - bcum + ic

    def step(carry, xs):
        cm, nv, m = carry
        qq, kq, vq, qkq, dm, dmx, bq, dl = xs
        inter = bq + m[..., None]
        mt = jnp.maximum(inter, dmx)
        w_int = jnp.exp(inter - mt)
        wqk = jnp.exp(dm - mt[..., None]) * qkq
        num = w_int[..., None] * jnp.einsum('bhik,bhkv->bhiv', qq, cm) + jnp.einsum('bhij,bhjv->bhiv', wqk, vq)
        den = w_int * jnp.einsum('bhik,bhk->bhi', qq, nv) + jnp.sum(wqk, axis=-1)
        h = num / jnp.maximum(jnp.abs(den), jnp.exp(-mt))[..., None]
        m_new = mt[..., -1]
        f_old = jnp.exp(bq[..., -1] + m - m_new)
        kw = kq * jnp.exp(dl - m_new[..., None])[..., None]
        cm = f_old[..., None, None] * cm + jnp.einsum('bhjk,bhjv->bhkv', kw, vq)
        nv = f_old[..., None] * nv + jnp.sum(kw, axis=2)
        return (cm, nv, m_new), h

    (c_fin, n_fin, m_fin), h = lax.scan(step, (c0, n0, m0), (qc, kc, vc, qk, dmat, dmax, bcum, dlast))
    return from_chunks(h), c_fin, n_fin, m_fin


def s5_ssm(u, x0_re, x0_im, a_re, a_im, b_re, b_im, c_re, c_im, d, log_step):
    f32 = jnp.float32
    B, L, _ = u.shape
    c = min(CHUNK, L)
    nc = L // c
    ug = u.reshape(B, L, S5_GROUPS, S5_CH)
    step = jnp.exp(log_step)[:, None]
    da_re, da_im = step * a_re, step * a_im
    mag = jnp.exp(da_re)
    ab_re, ab_im = mag * jnp.cos(da_im), mag * jnp.sin(da_im)
    den = a_re * a_re + a_im * a_im
    nr, ni = ab_re - 1.0, ab_im
    f_re = (nr * a_re + ni * a_im) / den
    f_im = (ni * a_re - nr * a_im) / den
    bb_re = f_re[..., None] * b_re - f_im[..., None] * b_im
    bb_im = f_re[..., None] * b_im + f_im[..., None] * b_re
    bu_re = jnp.einsum('blgc,gpc->blgp', ug, bb_re)
    bu_im = jnp.einsum('blgc,gpc->blgp', ug, bb_im)
    bu_re = bu_re.at[:, 0].add(ab_re * x0_re - ab_im * x0_im)
    bu_im = bu_im.at[:, 0].add(ab_re * x0_im + ab_im * x0_re)

    def power(n):
        mg = jnp.exp(n * da_re)
        return mg * jnp.cos(n * da_im), mg * jnp.sin(n * da_im)

    def combine(e1, e2):
        n1, r1, i1 = e1
        n2, r2, i2 = e2
        pr, pi = power(n2)
        return (n1 + n2, pr * r1 - pi * i1 + r2, pr * i1 + pi * r1 + i2)

    br = bu_re.reshape(B, nc, c, S5_GROUPS, S5_P)
    bi = bu_im.reshape(B, nc, c, S5_GROUPS, S5_P)
    n_loc = jnp.ones((B, nc, c, 1, 1), f32)
    _, lr, li = lax.associative_scan(combine, (n_loc, br, bi), axis=2)
    n_chk = jnp.full((B, nc, 1, 1), float(c), f32)
    _, sr, si = lax.associative_scan(combine, (n_chk, lr[:, :, -1], li[:, :, -1]), axis=1)
    cr = jnp.concatenate([jnp.zeros_like(sr[:, :1]), sr[:, :-1]], axis=1)[:, :, None]
    ci = jnp.concatenate([jnp.zeros_like(si[:, :1]), si[:, :-1]], axis=1)[:, :, None]
    pw_r, pw_i = power(jnp.arange(1, c + 1, dtype=f32)[:, None, None])
    xr = (lr + pw_r * cr - pw_i * ci).reshape(B, L, S5_GROUPS, S5_P)
    xi = (li + pw_r * ci + pw_i * cr).reshape(B, L, S5_GROUPS, S5_P)
    y = jnp.einsum('blgp,gcp->blgc', xr, c_re) - jnp.einsum('blgp,gcp->blgc', xi, c_im)
    y = y.reshape(B, L, W_GROUP) + d * u
    return y, sr[:, -1], si[:, -1]


def sb_block(args, ks, vs):
    qblk, start = args
    B, blk, H, dh = qblk.shape
    Lk = ks.shape[1]
    nk = Lk // SB_KB
    z = jnp.einsum('bqhd,bkhd->bhqk', qblk, ks) * (dh ** -0.5)
    valid = jnp.arange(Lk)[None, :] < (start + jnp.arange(blk))[:, None]
    ls = jax.nn.log_sigmoid(z)
    log_not = jnp.where(valid, ls - z, 0.0)
    ln = log_not.reshape(B, H, blk, nk, SB_KB)
    ii = jnp.arange(SB_KB)
    tri_kb = (ii[:, None] > ii[None, :]).astype(z.dtype)
    jj = jnp.arange(nk)
    tri_nk = (jj[:, None] > jj[None, :]).astype(z.dtype)
    within = jnp.einsum('bhqnj,js->bhqns', ln, tri_kb)
    suffix = jnp.einsum('bhqm,mn->bhqn', jnp.sum(ln, axis=-1), tri_nk)
    after = (within + suffix[..., None]).reshape(B, H, blk, Lk)
    a = jnp.where(valid, jnp.exp(ls + after), 0.0)
    return jnp.einsum('bhqk,bkhd->bqhd', a, vs)


def stick_breaking(q, k, v, q_offset):
    B, Lq, H, dh = q.shape
    Lk = k.shape[1]
    blk = min(SB_BLOCK, Lq)
    nb = Lq // blk
    n_seg = math.gcd(nb, SB_SEGMENTS)
    seg = Lq // n_seg
    pad = (-Lk) % SB_KB
    k = jnp.pad(k, ((0, 0), (0, pad), (0, 0), (0, 0)))
    v = jnp.pad(v, ((0, 0), (0, pad), (0, 0), (0, 0)))
    outs = []
    for s in range(n_seg):
        k_lim = -(-(q_offset + (s + 1) * seg) // SB_KB) * SB_KB
        qs = q[:, s * seg:(s + 1) * seg].reshape(B, seg // blk, blk, H, dh).transpose(1, 0, 2, 3, 4)
        starts = q_offset + s * seg + jnp.arange(seg // blk) * blk
        o = lax.map(functools.partial(sb_block, ks=k[:, :k_lim], vs=v[:, :k_lim]), (qs, starts))
        outs.append(o.transpose(1, 0, 2, 3, 4).reshape(B, seg, H, dh))
    return jnp.concatenate(outs, axis=1)


def hybrid_mixing(h, lw, ls):
    (w_in, conv_w, a_log, dt_bias, gdn_norm, b_i, b_f, ml_norm, a_re, a_im, b_re, b_im,
     c_re, c_im, s5_d, log_step, w_glu, b_glu, s5_norm, sb_norm, w_out) = lw
    (k_past, v_past, gdn_s, gdn_conv, ml_c, ml_n, ml_m, s5_re, s5_im) = ls
    f32 = jnp.float32
    B, L, _ = h.shape
    z = h @ w_in
    z_a, z_b, z_c, z_d = jnp.split(z, [GDN_IN, GDN_IN + ML_IN, GDN_IN + ML_IN + S5_IN], axis=-1)
    qkv, conv_new = causal_conv(z_a[..., :3 * W_GROUP], gdn_conv, conv_w)
    qkv = jax.nn.silu(qkv.astype(f32)).reshape(B, L, 3, GDN_HEADS, GDN_HD)
    q_a = l2_normalize(qkv[:, :, 0]) * GDN_HD ** -0.5
    k_a = l2_normalize(qkv[:, :, 1])
    v_a = qkv[:, :, 2]
    ga = z_a[..., 3 * W_GROUP:].astype(f32)
    log_a = -jnp.exp(a_log.astype(f32)) * jax.nn.softplus(ga[..., :GDN_HEADS] + dt_bias.astype(f32))
    beta = jax.nn.sigmoid(ga[..., GDN_HEADS:2 * GDN_HEADS])
    o_a, s_a = gated_delta_chunked(q_a, k_a, v_a, log_a, beta, gdn_s.astype(f32))
    g_out = ga[..., 2 * GDN_HEADS:].reshape(B, L, GDN_HEADS, GDN_HD)
    o_a = (rms_norm(o_a, gdn_norm) * jax.nn.silu(g_out)).reshape(B, L, W_GROUP)
    qkv_b = z_b[..., :3 * W_GROUP].astype(f32).reshape(B, L, 3, ML_HEADS, ML_HD)
    gb = z_b[..., 3 * W_GROUP:].astype(f32)
    i_pre = gb[..., :ML_HEADS] + b_i.astype(f32)
    log_f = jax.nn.log_sigmoid(gb[..., ML_HEADS:2 * ML_HEADS] + b_f.astype(f32))
    o_gate = jax.nn.sigmoid(gb[..., 2 * ML_HEADS:])
    h_b, c_b, n_b, m_b = mlstm_chunked(qkv_b[:, :, 0], qkv_b[:, :, 1] * ML_HD ** -0.5, qkv_b[:, :, 2],
                                       i_pre, log_f, ml_c.astype(f32), ml_n.astype(f32), ml_m.astype(f32))
    o_b = o_gate * rms_norm(h_b, ml_norm.reshape(ML_HEADS, ML_HD)).reshape(B, L, W_GROUP)
    y_c, x_re, x_im = s5_ssm(z_c.astype(f32), s5_re.astype(f32), s5_im.astype(f32),
                             a_re.astype(f32), a_im.astype(f32), b_re.astype(f32), b_im.astype(f32),
                             c_re.astype(f32), c_im.astype(f32), s5_d.astype(f32), log_step.astype(f32))
    y_c = jax.nn.gelu(y_c)
    o_c = rms_norm(y_c * jax.nn.sigmoid(y_c @ w_glu.astype(f32) + b_glu.astype(f32)), s5_norm)
    qkv_d = z_d.reshape(B, L, 3, SB_HEADS, SB_HD)
    k_new, v_new = qkv_d[:, :, 1], qkv_d[:, :, 2]
    k_all = jnp.concatenate([k_past.astype(k_new.dtype), k_new], axis=1).astype(f32)
    v_all = jnp.concatenate([v_past.astype(v_new.dtype), v_new], axis=1).astype(f32)
    o_d = stick_breaking(qkv_d[:, :, 0].astype(f32), k_all, v_all, k_past.shape[1])
    o_d = rms_norm(o_d.reshape(B, L, W_GROUP), sb_norm)
    mixed = jnp.concatenate([o_a, o_b, o_c, o_d], axis=-1).astype(h.dtype) @ w_out
    return mixed, (k_new, v_new, s_a, conv_new, c_b, n_b, m_b, x_re, x_im)


def run_trunk(x, states, weights):
    (sb_k, sb_v, gdn_s, gdn_conv, ml_c, ml_n, ml_m, s5_re, s5_im) = states
    (ffn1_norm, ffn1_w_gate, ffn1_w_up, ffn1_w_down, mix_norm, w_in, gdn_conv_w, gdn_a_log,
     gdn_dt_bias, gdn_norm, ml_b_i, ml_b_f, ml_norm, s5_a_re, s5_a_im, s5_b_re, s5_b_im,
     s5_c_re, s5_c_im, s5_d, s5_log_step, s5_w_glu, s5_b_glu, s5_norm, sb_norm, w_out,
     ffn2_norm, ffn2_w_gate, ffn2_w_up, ffn2_w_down, final_norm) = weights
    new = [[] for _ in range(9)]
    for l in range(DEPTH):
        x = x + 0.5 * swiglu_ffn(rms_norm(x, ffn1_norm[l]), ffn1_w_gate[l], ffn1_w_up[l], ffn1_w_down[l])
        lw = (w_in[l], gdn_conv_w[l], gdn_a_log[l], gdn_dt_bias[l], gdn_norm[l], ml_b_i[l], ml_b_f[l],
              ml_norm[l], s5_a_re[l], s5_a_im[l], s5_b_re[l], s5_b_im[l], s5_c_re[l], s5_c_im[l], s5_d[l],
              s5_log_step[l], s5_w_glu[l], s5_b_glu[l], s5_norm[l], sb_norm[l], w_out[l])
        ls = (sb_k[l], sb_v[l], gdn_s[l], gdn_conv[l], ml_c[l], ml_n[l], ml_m[l], s5_re[l], s5_im[l])
        mixed, layer_new = hybrid_mixing(rms_norm(x, mix_norm[l]), lw, ls)
        x = x + mixed
        x = x + 0.5 * swiglu_ffn(rms_norm(x, ffn2_norm[l]), ffn2_w_gate[l], ffn2_w_up[l], ffn2_w_down[l])
        for lst, arr in zip(new, layer_new):
            lst.append(arr)
    return rms_norm(x, final_norm), [jnp.stack(lst) for lst in new]


def setup_inputs(seed: int = 0) -> dict:
    key = jax.random.key(seed)
    ks = iter(jax.random.split(key, 64))
    nrm = lambda shape, s=1.0: jax.random.normal(next(ks), shape, jnp.float32) * s
    gain = lambda shape: 1.0 + 0.02 * jax.random.normal(next(ks), shape, jnp.float32)
    unif = lambda shape, lo, hi: jax.random.uniform(next(ks), shape, jnp.float32, lo, hi)
    dt = jnp.exp(unif((DEPTH, GDN_HEADS), math.log(1e-3), math.log(1e-1)))
    return {
        "x_prompt": nrm((BATCH, SEQ, D_MODEL)),
        "x_sample": nrm((DEC_BATCH, DEC_SEQ, D_MODEL)),
        "cache_sb_k": nrm((DEPTH, DEC_BATCH, PAST_LEN, SB_HEADS, SB_HD)),
        "cache_sb_v": nrm((DEPTH, DEC_BATCH, PAST_LEN, SB_HEADS, SB_HD)),
        "state_gdn_s": nrm((DEPTH, DEC_BATCH, GDN_HEADS, GDN_HD, GDN_HD), 0.1),
        "state_gdn_conv": nrm((DEPTH, DEC_BATCH, CONV_W - 1, 3 * W_GROUP)),
        "state_mlstm_c": nrm((DEPTH, DEC_BATCH, ML_HEADS, ML_HD, ML_HD), 0.1),
        "state_mlstm_n": nrm((DEPTH, DEC_BATCH, ML_HEADS, ML_HD), 0.1),
        "state_mlstm_m": nrm((DEPTH, DEC_BATCH, ML_HEADS)),
        "state_s5_re": nrm((DEPTH, DEC_BATCH, S5_GROUPS, S5_P), 0.1),
        "state_s5_im": nrm((DEPTH, DEC_BATCH, S5_GROUPS, S5_P), 0.1),
        "ffn1_norm": gain((DEPTH, D_MODEL)),
        "ffn1_w_gate": nrm((DEPTH, D_MODEL, D_FF), D_MODEL ** -0.5),
        "ffn1_w_up": nrm((DEPTH, D_MODEL, D_FF), D_MODEL ** -0.5),
        "ffn1_w_down": nrm((DEPTH, D_FF, D_MODEL), D_FF ** -0.5),
        "mix_norm": gain((DEPTH, D_MODEL)),
        "w_in": nrm((DEPTH, D_MODEL, N_IN), D_MODEL ** -0.5),
        "gdn_conv_w": nrm((DEPTH, CONV_W, 3 * W_GROUP), CONV_W ** -0.5),
        "gdn_a_log": jnp.log(unif((DEPTH, GDN_HEADS), 1.0, 16.0)),
        "gdn_dt_bias": dt + jnp.log(-jnp.expm1(-dt)),
        "gdn_norm": gain((DEPTH, GDN_HD)),
        "ml_b_i": nrm((DEPTH, ML_HEADS), 0.1),
        "ml_b_f": jnp.linspace(3.0, 6.0, ML_HEADS, dtype=jnp.float32)[None, :] + nrm((DEPTH, ML_HEADS), 0.1),
        "ml_norm": gain((DEPTH, W_GROUP)),
        "s5_a_re": -0.5 + nrm((DEPTH, S5_GROUPS, S5_P), 0.01),
        "s5_a_im": math.pi * jnp.arange(S5_P, dtype=jnp.float32) + nrm((DEPTH, S5_GROUPS, S5_P), 0.01),
        "s5_b_re": nrm((DEPTH, S5_GROUPS, S5_P, S5_CH), (2 * S5_CH) ** -0.5),
        "s5_b_im": nrm((DEPTH, S5_GROUPS, S5_P, S5_CH), (2 * S5_CH) ** -0.5),
        "s5_c_re": nrm((DEPTH, S5_GROUPS, S5_CH, S5_P), (2 * S5_P) ** -0.5),
        "s5_c_im": nrm((DEPTH, S5_GROUPS, S5_CH, S5_P), (2 * S5_P) ** -0.5),
        "s5_d": nrm((DEPTH, W_GROUP)),
        "s5_log_step": unif((DEPTH, S5_GROUPS), math.log(1e-3), math.log(1e-1)),
        "s5_w_glu": nrm((DEPTH, W_GROUP, W_GROUP), W_GROUP ** -0.5),
        "s5_b_glu": nrm((DEPTH, W_GROUP), 0.01),
        "s5_norm": gain((DEPTH, W_GROUP)),
        "sb_norm": gain((DEPTH, W_GROUP)),
        "w_out": nrm((DEPTH, D_MIX, D_MODEL), D_MIX ** -0.5),
        "ffn2_norm": gain((DEPTH, D_MODEL)),
        "ffn2_w_gate": nrm((DEPTH, D_MODEL, D_FF), D_MODEL ** -0.5),
        "ffn2_w_up": nrm((DEPTH, D_MODEL, D_FF), D_MODEL ** -0.5),
        "ffn2_w_down": nrm((DEPTH, D_FF, D_MODEL), D_FF ** -0.5),
        "final_norm": gain((D_MODEL,)),
    }


def reference(x_prompt, x_sample, cache_sb_k, cache_sb_v, state_gdn_s, state_gdn_conv, state_mlstm_c,
              state_mlstm_n, state_mlstm_m, state_s5_re, state_s5_im,
              ffn1_norm, ffn1_w_gate, ffn1_w_up, ffn1_w_down, mix_norm, w_in, gdn_conv_w, gdn_a_log,
              gdn_dt_bias, gdn_norm, ml_b_i, ml_b_f, ml_norm, s5_a_re, s5_a_im, s5_b_re, s5_b_im,
              s5_c_re, s5_c_im, s5_d, s5_log_step, s5_w_glu, s5_b_glu, s5_norm, sb_norm, w_out,
              ffn2_norm, ffn2_w_gate, ffn2_w_up, ffn2_w_down, final_norm):
    weights = (ffn1_norm, ffn1_w_gate, ffn1_w_up, ffn1_w_down, mix_norm, w_in, gdn_conv_w, gdn_a_log,
               gdn_dt_bias, gdn_norm, ml_b_i, ml_b_f, ml_norm, s5_a_re, s5_a_im, s5_b_re, s5_b_im,
               s5_c_re, s5_c_im, s5_d, s5_log_step, s5_w_glu, s5_b_glu, s5_norm, sb_norm, w_out,
               ffn2_norm, ffn2_w_gate, ffn2_w_up, ffn2_w_down, final_norm)
    f32 = jnp.float32
    dtp = x_prompt.dtype
    fresh = (jnp.zeros((DEPTH, BATCH, 0, SB_HEADS, SB_HD), dtp),
             jnp.zeros((DEPTH, BATCH, 0, SB_HEADS, SB_HD), dtp),
             jnp.zeros((DEPTH, BATCH, GDN_HEADS, GDN_HD, GDN_HD), f32),
             jnp.zeros((DEPTH, BATCH, CONV_W - 1, 3 * W_GROUP), dtp),
             jnp.zeros((DEPTH, BATCH, ML_HEADS, ML_HD, ML_HD), f32),
             jnp.zeros((DEPTH, BATCH, ML_HEADS, ML_HD), f32),
             jnp.zeros((DEPTH, BATCH, ML_HEADS), f32),
             jnp.zeros((DEPTH, BATCH, S5_GROUPS, S5_P), f32),
             jnp.zeros((DEPTH, BATCH, S5_GROUPS, S5_P), f32))
    y_prompt, (p_sb_k, p_sb_v, p_gdn_s, p_gdn_conv, p_ml_c, p_ml_n, p_ml_m, p_s5_re, p_s5_im) = \
        run_trunk(x_prompt, fresh, weights)
    running = (cache_sb_k, cache_sb_v, state_gdn_s, state_gdn_conv, state_mlstm_c, state_mlstm_n,
               state_mlstm_m, state_s5_re, state_s5_im)
    y_sample, (s_sb_k, s_sb_v, s_gdn_s, s_gdn_conv, s_ml_c, s_ml_n, s_ml_m, s_s5_re, s_s5_im) = \
        run_trunk(x_sample, running, weights)
    return (y_prompt, y_sample,
            p_sb_k, p_sb_v, p_gdn_s, p_gdn_conv, p_ml_c, p_ml_n, p_ml_m, p_s5_re, p_s5_im,
            s_sb_k, s_sb_v, s_gdn_s, s_gdn_conv, s_ml_c, s_ml_n, s_ml_m, s_s5_re, s_s5_im)
```

```python
import functools

import numpy as np
import jax
import jax.numpy as jnp
from jax import lax
from jax.experimental import pallas as pl
from jax.experimental.pallas import tpu as pltpu

F32 = jnp.float32
BF16 = jnp.bfloat16
HI = lax.Precision.HIGHEST
EPS = 1e-6

HEADS = 4
HD = 64
WG = HEADS * HD
LANES = 128
CONV_W = 4
S5_GROUPS, S5_CH, S5_P = 16, 16, 64
S5_N = S5_GROUPS * S5_P
NEG_INF = float("-inf")

Z_A_QKV, Z_D_QKV, Z_B_QKV = 0, 768, 1536
Z_A_GOUT, Z_B_OG, Z_C = 2304, 2560, 2816
Z_S1, Z_S2 = 3072, 3200
NZ = 3328
ML_LANE, GDN_LANE = 0, 4

VMEM_LIMIT = 56 * 1024 * 1024


def _dot(a, b, precision=None):
    return jnp.dot(a, b, preferred_element_type=F32, precision=precision)


def _dot_nt(a, b, precision=None):
    return lax.dot_general(a, b, (((1,), (1,)), ((), ())), preferred_element_type=F32, precision=precision)


def _dot_tn(a, b, precision=None):
    return lax.dot_general(a, b, (((0,), (0,)), ((), ())), preferred_element_type=F32, precision=precision)


def _rms(x, g):
    return x * lax.rsqrt(jnp.mean(x * x, axis=-1, keepdims=True) + EPS) * g


def _head_masks(dtype=F32):
    lane = lax.broadcasted_iota(jnp.int32, (1, WG), 1)
    return [((lane >= h * HD) & (lane < (h + 1) * HD)).astype(dtype) for h in range(HEADS)]


def _block_diag_mask():
    shift = HD.bit_length() - 1
    r = jnp.right_shift(lax.broadcasted_iota(jnp.int32, (WG, WG), 0), shift)
    c = jnp.right_shift(lax.broadcasted_iota(jnp.int32, (WG, WG), 1), shift)
    return (r == c).astype(F32)


def _tri(c, strict):
    r = lax.broadcasted_iota(jnp.int32, (c, c), 0)
    col = lax.broadcasted_iota(jnp.int32, (c, c), 1)
    return (r > col) if strict else (r >= col)


def _ffn_body(*refs, n_f, has_mix, has_final):
    it = iter(refs)
    x_ref = next(it)
    if has_mix:
        o_refs = [next(it) for _ in range(4)]
        wo_ref = next(it)
    g_ref, wg_ref, wu_ref, wd_ref = next(it), next(it), next(it), next(it)
    gf_ref = next(it) if has_final else None
    out_ref = next(it)
    xin_scr, h_scr, acc_scr = next(it), next(it), next(it)
    j = pl.program_id(1)

    @pl.when(j == 0)
    def _():
        x = x_ref[...]
        if has_mix:
            for m, o_ref in enumerate(o_refs):
                x = x + _dot(o_ref[...].astype(BF16), wo_ref[m * WG:(m + 1) * WG, :])
        xin_scr[...] = x
        h_scr[...] = _rms(x, g_ref[...]).astype(BF16)
        acc_scr[...] = jnp.zeros_like(acc_scr)

    h = h_scr[...]
    a = _dot(h, wg_ref[...])
    b = _dot(h, wu_ref[...])
    t = (a * jax.nn.sigmoid(a)) * b
    acc_scr[...] += _dot(t.astype(BF16), wd_ref[...])

    @pl.when(j == n_f - 1)
    def _():
        y = xin_scr[...] + 0.5 * acc_scr[...]
        if has_final:
            y = _rms(y, gf_ref[...])
        out_ref[...] = y


def _ffn(x, g, wg, wu, wd, mix=None, final_g=None, *, tm=512, tf=512):
    t, d = x.shape
    f = wg.shape[1]
    n_f = f // tf
    row = lambda i, j: (i, 0)
    const = lambda i, j: (0, 0)
    in_specs = [pl.BlockSpec((tm, d), row)]
    args = [x]
    if mix is not None:
        in_specs += [pl.BlockSpec((tm, WG), row)] * 4 + [pl.BlockSpec(mix[4].shape, const)]
        args += list(mix)
    in_specs += [pl.BlockSpec((1, d), const), pl.BlockSpec((d, tf), lambda i, j: (0, j)),
                 pl.BlockSpec((d, tf), lambda i, j: (0, j)), pl.BlockSpec((tf, d), lambda i, j: (j, 0))]
    args += [g, wg, wu, wd]
    if final_g is not None:
        in_specs.append(pl.BlockSpec((1, d), const))
        args.append(final_g)
    return pl.pallas_call(
        functools.partial(_ffn_body, n_f=n_f, has_mix=mix is not None, has_final=final_g is not None),
        grid=(t // tm, n_f),
        in_specs=in_specs,
        out_specs=pl.BlockSpec((tm, d), row),
        out_shape=jax.ShapeDtypeStruct((t, d), F32),
        scratch_shapes=[pltpu.VMEM((tm, d), F32), pltpu.VMEM((tm, d), BF16), pltpu.VMEM((tm, d), F32)],
        compiler_params=pltpu.CompilerParams(dimension_semantics=("parallel", "arbitrary"),
                                             vmem_limit_bytes=VMEM_LIMIT),
        name="ffn",
    )(*args)


def _proj_body(x_ref, g_ref, w_ref, z_ref, zd_ref, h_scr):
    j = pl.program_id(1)

    @pl.when(j == 0)
    def _():
        h_scr[...] = _rms(x_ref[...], g_ref[...]).astype(BF16)

    z = _dot(h_scr[...], w_ref[...])
    z_ref[...] = z

    @pl.when(j == 0)
    def _():
        zd_ref[...] = z[:, Z_D_QKV:Z_D_QKV + 3 * WG].astype(BF16)


def _proj(x, g, w, *, tm=512, tn=NZ // 2):
    t, d = x.shape
    assert Z_D_QKV + 3 * WG <= tn
    return pl.pallas_call(
        _proj_body,
        grid=(t // tm, NZ // tn),
        in_specs=[pl.BlockSpec((tm, d), lambda i, j: (i, 0)), pl.BlockSpec((1, d), lambda i, j: (0, 0)),
                  pl.BlockSpec((d, tn), lambda i, j: (0, j))],
        out_specs=[pl.BlockSpec((tm, tn), lambda i, j: (i, j)), pl.BlockSpec((tm, 3 * WG), lambda i, j: (i, 0))],
        out_shape=[jax.ShapeDtypeStruct((t, NZ), F32), jax.ShapeDtypeStruct((t, 3 * WG), BF16)],
        scratch_shapes=[pltpu.VMEM((tm, d), BF16)],
        compiler_params=pltpu.CompilerParams(dimension_semantics=("parallel", "arbitrary"),
                                             vmem_limit_bytes=VMEM_LIMIT),
        name="proj",
    )(x, g, w)


INV_BASE = 8


def _unit_lower_inverse(m, c):
    r = lax.broadcasted_iota(jnp.int32, (c, c), 0)
    col = lax.broadcasted_iota(jnp.int32, (c, c), 1)
    same = lambda s: jnp.right_shift(r, s.bit_length() - 1) == jnp.right_shift(col, s.bit_length() - 1)
    p = jnp.where(same(INV_BASE), -m, 0.0)
    x = (r == col).astype(F32) + p
    n = 2
    while n < INV_BASE:
        p = _dot(p, p)
        x = x + _dot(x, p)
        n *= 2
    s = INV_BASE
    while s < c:
        join = jnp.where(same(2 * s) & jnp.logical_not(same(s)), m, 0.0)
        x = x - _dot(x, _dot(join, x))
        s *= 2
    return x


def _gdn_chunk(qkv, s1, s2, s_bd, alog, dtb, hm, bd, c):
    q, k, v = qkv[:, 0:WG], qkv[:, WG:2 * WG], qkv[:, 2 * WG:3 * WG]
    q = q * lax.rsqrt(_dot(q * q, bd, HI) + EPS) * (HD ** -0.5)
    k = k * lax.rsqrt(_dot(k * k, bd, HI) + EPS)
    log_a = -jnp.exp(alog) * jax.nn.softplus(s1 + dtb)
    beta = jax.nn.sigmoid(s2)
    incl, strict = _tri(c, False), _tri(c, True)
    gc = _dot(incl.astype(F32), log_a, HI)
    gc_t = gc.T
    wv = jnp.zeros((c, WG), F32)
    wk = jnp.zeros((c, WG), F32)
    qg = jnp.zeros((c, WG), F32)
    kg = jnp.zeros((c, WG), F32)
    gl = jnp.zeros((1, WG), F32)
    ps = []
    for h in range(HEADS):
        l = GDN_LANE + h
        gcol, grow, bcol = gc[:, l:l + 1], gc_t[l:l + 1, :], beta[:, l:l + 1]
        decay = jnp.exp(jnp.where(incl, gcol - grow, NEG_INF))
        km, qm = k * hm[h], q * hm[h]
        m = jnp.where(strict, bcol * _dot_nt(km, k) * decay, 0.0)
        t = _unit_lower_inverse(m, c)
        gam = jnp.exp(gcol)
        wv = wv + hm[h] * _dot(t, bcol * v)
        wk = wk + _dot(t, (bcol * gam) * km)
        ps.append(_dot_nt(qm, k) * decay)
        qg = qg + qm * gam
        glast = gc[c - 1:c, l:l + 1]
        kg = kg + km * jnp.exp(glast - gcol)
        gl = gl + hm[h] * jnp.exp(glast)
    u = wv - _dot(wk, s_bd)
    o = _dot(qg, s_bd)
    for h in range(HEADS):
        o = o + hm[h] * _dot(ps[h], u)
    s_bd = s_bd * gl + bd * _dot_tn(kg, u)
    return o, s_bd


def _gdn_body(zqkv, zgo, s1r, s2r, conv8, s0, cw, alog, dtb, gn, o_ref, sfin_ref, xbuf, s_scr, *, c, g, n_steps):
    i = pl.program_id(1)
    r = c * g

    @pl.when(i == 0)
    def _():
        xbuf[0:8, :] = conv8[0]
        s_scr[...] = s0[0]

    xbuf[8:8 + r, :] = zqkv[...]
    w = cw[...]
    y = (xbuf[8:8 + r, :] * w[3:4, :] + xbuf[7:7 + r, :] * w[2:3, :]
         + xbuf[6:6 + r, :] * w[1:2, :] + xbuf[5:5 + r, :] * w[0:1, :])
    xbuf[0:8, :] = xbuf[r:r + 8, :]
    qkv = y * jax.nn.sigmoid(y)
    hm, bd = _head_masks(), _block_diag_mask()
    s_bd = s_scr[...]
    for ci in range(g):
        rows = slice(ci * c, (ci + 1) * c)
        o, s_bd = _gdn_chunk(qkv[rows], s1r[rows, :], s2r[rows, :], s_bd, alog[...], dtb[...], hm, bd, c)
        on = o * lax.rsqrt(_dot(o * o, bd, HI) * (1.0 / HD) + EPS) * gn[...]
        gate = zgo[rows, :]
        o_ref[rows, :] = on * (gate * jax.nn.sigmoid(gate))
    s_scr[...] = s_bd

    @pl.when(i == n_steps - 1)
    def _():
        sfin_ref[0] = s_bd


def _stream_specs(row0, r, n_steps, cols):
    return [pl.BlockSpec((r, w), functools.partial(lambda s, i, cb: (row0 // r + s * n_steps + i, cb), cb=off // w))
            for off, w in cols]


def _per_stream(shape):
    nd = len(shape)
    return pl.BlockSpec((1,) + tuple(shape[1:]), lambda s, i: (s,) + (0,) * (nd - 1))


def _const2(shape):
    return pl.BlockSpec(shape, lambda s, i: (0, 0))


def _gdn(z, row0, n_streams, length, c, g, conv8, s0, cw, alog, dtb, gn):
    r = c * g
    n_steps = length // r
    in_specs = _stream_specs(row0, r, n_steps, [(Z_A_QKV, 3 * WG), (Z_A_GOUT, WG), (Z_S1, LANES), (Z_S2, LANES)])
    in_specs += [_per_stream(conv8.shape), _per_stream(s0.shape),
                 _const2(cw.shape), _const2(alog.shape), _const2(dtb.shape), _const2(gn.shape)]
    return pl.pallas_call(
        functools.partial(_gdn_body, c=c, g=g, n_steps=n_steps),
        grid=(n_streams, n_steps),
        in_specs=in_specs,
        out_specs=[pl.BlockSpec((r, WG), lambda s, i: (s * n_steps + i, 0)), _per_stream(s0.shape)],
        out_shape=[jax.ShapeDtypeStruct((n_streams * length, WG), F32), jax.ShapeDtypeStruct(s0.shape, F32)],
        scratch_shapes=[pltpu.VMEM((r + 8, 3 * WG), F32), pltpu.VMEM((WG, WG), F32)],
        compiler_params=pltpu.CompilerParams(dimension_semantics=("parallel", "arbitrary"),
                                             vmem_limit_bytes=VMEM_LIMIT),
        name="gdn",
    )(z, z, z, z, conv8, s0, cw, alog, dtb, gn)


def _mlstm_chunk(qkv, s1, s2, c_bd, n_row, m_vec, b_i, b_f, hm, bd, c):
    q, k, v = qkv[:, 0:WG], qkv[:, WG:2 * WG] * (HD ** -0.5), qkv[:, 2 * WG:3 * WG]
    i_pre = s1 + b_i
    log_f = jax.nn.log_sigmoid(s2 + b_f)
    incl = _tri(c, False)
    bcum = _dot(incl.astype(F32), log_f, HI)
    d_t = (i_pre - bcum).T
    lane = lax.broadcasted_iota(jnp.int32, (1, LANES), 1)
    hsum = jnp.zeros((c, WG), F32)
    kw_all = jnp.zeros((c, WG), F32)
    f_lanes = jnp.zeros((1, WG), F32)
    m_next = jnp.zeros((1, LANES), F32)
    for h in range(HEADS):
        l = ML_LANE + h
        bcol, drow, icol = bcum[:, l:l + 1], d_t[l:l + 1, :], i_pre[:, l:l + 1]
        m_h = m_vec[:, l:l + 1]
        dmat = jnp.where(incl, bcol + drow, NEG_INF)
        dmax = jnp.max(dmat, axis=-1, keepdims=True)
        inter = bcol + m_h
        mt = jnp.maximum(inter, dmax)
        w_int = jnp.exp(inter - mt)
        qm, km = q * hm[h], k * hm[h]
        wqk = jnp.exp(dmat - mt) * _dot_nt(qm, k)
        num = w_int * _dot(qm, c_bd) + hm[h] * _dot(wqk, v)
        den = w_int * jnp.sum(qm * n_row, axis=-1, keepdims=True) + jnp.sum(wqk, axis=-1, keepdims=True)
        hsum = hsum + num / jnp.maximum(jnp.abs(den), jnp.exp(-mt))
        m_new = mt[c - 1:c, :]
        blast = bcum[c - 1:c, l:l + 1]
        f_old = jnp.exp(blast + m_h - m_new)
        kw_all = kw_all + km * jnp.exp(blast - bcol + icol - m_new)
        f_lanes = f_lanes + hm[h] * f_old
        m_next = m_next + jnp.where(lane == l, m_new, 0.0)
    c_bd = c_bd * f_lanes + bd * _dot_tn(kw_all, v)
    n_row = n_row * f_lanes + jnp.sum(kw_all, axis=0, keepdims=True)
    return hsum, c_bd, n_row, m_next


def _mlstm_body(zqkv, zog, s1r, s2r, c0, n0, m0, b_i, b_f, gn, o_ref, cfin, nfin, mfin, c_scr, n_scr, m_scr,
                *, c, g, n_steps):
    i = pl.program_id(1)

    @pl.when(i == 0)
    def _():
        c_scr[...] = c0[0]
        n_scr[...] = n0[0]
        m_scr[...] = m0[0]

    hm, bd = _head_masks(), _block_diag_mask()
    c_bd, n_row, m_vec = c_scr[...], n_scr[...], m_scr[...]
    for ci in range(g):
        rows = slice(ci * c, (ci + 1) * c)
        hh, c_bd, n_row, m_vec = _mlstm_chunk(zqkv[rows, :], s1r[rows, :], s2r[rows, :], c_bd, n_row, m_vec,
                                              b_i[...], b_f[...], hm, bd, c)
        hn = hh * lax.rsqrt(_dot(hh * hh, bd, HI) * (1.0 / HD) + EPS) * gn[...]
        o_ref[rows, :] = jax.nn.sigmoid(zog[rows, :]) * hn
    c_scr[...] = c_bd
    n_scr[...] = n_row
    m_scr[...] = m_vec

    @pl.when(i == n_steps - 1)
    def _():
        cfin[0] = c_bd
        nfin[0] = n_row
        mfin[0] = m_vec


def _mlstm(z, row0, n_streams, length, c, g, c0, n0, m0, b_i, b_f, gn):
    r = c * g
    n_steps = length // r
    in_specs = _stream_specs(row0, r, n_steps, [(Z_B_QKV, 3 * WG), (Z_B_OG, WG), (Z_S1, LANES), (Z_S2, LANES)])
    in_specs += [_per_stream(c0.shape), _per_stream(n0.shape), _per_stream(m0.shape),
                 _const2(b_i.shape), _const2(b_f.shape), _const2(gn.shape)]
    return pl.pallas_call(
        functools.partial(_mlstm_body, c=c, g=g, n_steps=n_steps),
        grid=(n_streams, n_steps),
        in_specs=in_specs,
        out_specs=[pl.BlockSpec((r, WG), lambda s, i: (s * n_steps + i, 0)),
                   _per_stream(c0.shape), _per_stream(n0.shape), _per_stream(m0.shape)],
        out_shape=[jax.ShapeDtypeStruct((n_streams * length, WG), F32), jax.ShapeDtypeStruct(c0.shape, F32),
                   jax.ShapeDtypeStruct(n0.shape, F32), jax.ShapeDtypeStruct(m0.shape, F32)],
        scratch_shapes=[pltpu.VMEM((WG, WG), F32), pltpu.VMEM((1, WG), F32), pltpu.VMEM((1, LANES), F32)],
        compiler_params=pltpu.CompilerParams(dimension_semantics=("parallel", "arbitrary"),
                                             vmem_limit_bytes=VMEM_LIMIT),
        name="mlstm",
    )(z, z, z, z, c0, n0, m0, b_i, b_f, gn)


def _s5_body(zc, x0r, x0i, a_re, a_im, step, b_re, b_im, c_re, c_im, d_row, wglu, bglu, gn,
             o_ref, xfr, xfi, par, st, xs_r, xs_i, *, r, n_steps):
    i = pl.program_id(1)

    @pl.when(i == 0)
    def _():
        ar, ai = a_re[...], a_im[...]
        mag = jnp.exp(step[...] * ar)
        abr, abi = mag * jnp.cos(step[...] * ai), mag * jnp.sin(step[...] * ai)
        den = ar * ar + ai * ai
        nr, ni = abr - 1.0, abi
        par[0:1, :] = abr
        par[1:2, :] = abi
        par[2:3, :] = (nr * ar + ni * ai) / den
        par[3:4, :] = (ni * ar - nr * ai) / den
        st[0:1, :] = x0r[0]
        st[1:2, :] = x0i[0]

    abr, abi, f_re, f_im = par[0:1, :], par[1:2, :], par[2:3, :], par[3:4, :]
    u = zc[...]
    ub = u.astype(BF16)
    pr, pi = _dot(ub, b_re[...]), _dot(ub, b_im[...])
    bu_r = f_re * pr - f_im * pi
    bu_i = f_re * pi + f_im * pr
    xr, xi = st[0:1, :], st[1:2, :]
    for t in range(r):
        xr, xi = abr * xr - abi * xi + bu_r[t:t + 1, :], abr * xi + abi * xr + bu_i[t:t + 1, :]
        xs_r[t:t + 1, :] = xr
        xs_i[t:t + 1, :] = xi
    st[0:1, :] = xr
    st[1:2, :] = xi
    y = _dot(xs_r[...].astype(BF16), c_re[...]) - _dot(xs_i[...].astype(BF16), c_im[...]) + d_row[...] * u
    y = jax.nn.gelu(y)
    gate = jax.nn.sigmoid(_dot(y.astype(BF16), wglu[...]) + bglu[...])
    o_ref[...] = _rms(y * gate, gn[...])

    @pl.when(i == n_steps - 1)
    def _():
        xfr[0] = xr
        xfi[0] = xi


def _s5(z, row0, n_streams, length, r, x0r, x0i, a_re, a_im, step, b_re, b_im, c_re, c_im, d_row, wglu, bglu, gn):
    n_steps = length // r
    in_specs = _stream_specs(row0, r, n_steps, [(Z_C, WG)])
    in_specs += [_per_stream(x0r.shape), _per_stream(x0i.shape)]
    consts = [a_re, a_im, step, b_re, b_im, c_re, c_im, d_row, wglu, bglu, gn]
    in_specs += [_const2(a.shape) for a in consts]
    return pl.pallas_call(
        functools.partial(_s5_body, r=r, n_steps=n_steps),
        grid=(n_streams, n_steps),
        in_specs=in_specs,
        out_specs=[pl.BlockSpec((r, WG), lambda s, i: (s * n_steps + i, 0)),
                   _per_stream(x0r.shape), _per_stream(x0i.shape)],
        out_shape=[jax.ShapeDtypeStruct((n_streams * length, WG), F32), jax.ShapeDtypeStruct(x0r.shape, F32),
                   jax.ShapeDtypeStruct(x0i.shape, F32)],
        scratch_shapes=[pltpu.VMEM((8, S5_N), F32), pltpu.VMEM((8, S5_N), F32),
                        pltpu.VMEM((r, S5_N), F32), pltpu.VMEM((r, S5_N), F32)],
        compiler_params=pltpu.CompilerParams(dimension_semantics=("parallel", "arbitrary"),
                                             vmem_limit_bytes=VMEM_LIMIT),
        name="s5",
    )(z, x0r, x0i, *consts)


SB_KB = 256


def _stack_heads(q):
    lane = lax.broadcasted_iota(jnp.int32, (1, WG), 1)
    return jnp.concatenate([jnp.where((lane >= h * HD) & (lane < (h + 1) * HD), q, jnp.zeros_like(q))
                            for h in range(HEADS)], axis=0)


def _unstack_heads(o4, n):
    hm = _head_masks()
    return sum(hm[h] * o4[h * n:(h + 1) * n, :] for h in range(HEADS))


def _sb_tile(q4, kblk, vblk, carry, tri, valid):
    z = _dot_nt(q4, kblk) * (HD ** -0.5)
    ls = jnp.minimum(z, 0.0) - jnp.log1p(jnp.exp(-jnp.abs(z)))
    ln = ls - z
    if valid is not None:
        ln = jnp.where(valid, ln, 0.0)
    after = _dot(ln.astype(BF16), tri) + carry
    a = jnp.exp(ls + after)
    if valid is not None:
        a = jnp.where(valid, a, 0.0)
    return _dot(a.astype(BF16), vblk), jnp.sum(ln, axis=-1, keepdims=True)


def _suffix_tri():
    j = lax.broadcasted_iota(jnp.int32, (SB_KB, SB_KB), 0)
    s = lax.broadcasted_iota(jnp.int32, (SB_KB, SB_KB), 1)
    return (j > s).astype(BF16)


def _sbp_body(q_ref, k_ref, v_ref, gn, o_ref, acc, car, *, qb):
    i = pl.program_id(0)
    q4 = _stack_heads(q_ref[...])
    rows = HEADS * qb
    qpos = i * qb + (lax.broadcasted_iota(jnp.int32, (rows, 1), 0) & (qb - 1))
    tri = _suffix_tri()
    top = (i * qb + qb - 2) // SB_KB
    k0 = pl.multiple_of(top * SB_KB, SB_KB)
    kpos = k0 + lax.broadcasted_iota(jnp.int32, (1, SB_KB), 1)
    do, dc = _sb_tile(q4, k_ref[pl.ds(k0, SB_KB), :], v_ref[pl.ds(k0, SB_KB), :], 0.0, tri, kpos < qpos)
    acc[...] = do
    car[...] = dc

    def step(n, _):
        kk = pl.multiple_of((top - 1 - n) * SB_KB, SB_KB)
        do, dc = _sb_tile(q4, k_ref[pl.ds(kk, SB_KB), :], v_ref[pl.ds(kk, SB_KB), :], car[...], tri, None)
        acc[...] += do
        car[...] += dc
        return 0

    lax.fori_loop(0, top, step, 0)
    o_ref[...] = _rms(_unstack_heads(acc[...], qb), gn[...])


def _sb_prompt(zd, length, gn, *, qb=128):
    rows = HEADS * qb
    return pl.pallas_call(
        functools.partial(_sbp_body, qb=qb),
        grid=(length // qb,),
        in_specs=[pl.BlockSpec((qb, WG), lambda i: (i, 0)), pl.BlockSpec((length, WG), lambda i: (0, 1)),
                  pl.BlockSpec((length, WG), lambda i: (0, 2)), pl.BlockSpec((1, WG), lambda i: (0, 0))],
        out_specs=pl.BlockSpec((qb, WG), lambda i: (i, 0)),
        out_shape=jax.ShapeDtypeStruct((length, WG), F32),
        scratch_shapes=[pltpu.VMEM((rows, WG), F32), pltpu.VMEM((rows, 1), F32)],
        compiler_params=pltpu.CompilerParams(dimension_semantics=("arbitrary",), vmem_limit_bytes=VMEM_LIMIT),
        name="sb_prompt",
    )(zd, zd, zd, gn)


def _sbs_body(q_ref, kn_ref, vn_ref, kc_ref, vc_ref, gn, o_ref, acc, car, *, lq, past):
    q4 = _stack_heads(q_ref[...])
    rows = HEADS * lq
    qpos = past + (lax.broadcasted_iota(jnp.int32, (rows, 1), 0) & (lq - 1))
    tri = _suffix_tri()
    pad = jnp.zeros((SB_KB - lq, WG), BF16)
    kpos = past + lax.broadcasted_iota(jnp.int32, (1, SB_KB), 1)
    do, dc = _sb_tile(q4, jnp.concatenate([kn_ref[...], pad], axis=0), jnp.concatenate([vn_ref[...], pad], axis=0),
                      0.0, tri, kpos < qpos)
    acc[...] = do
    car[...] = dc
    n_tiles = past // SB_KB

    def step(n, _):
        kk = pl.multiple_of((n_tiles - 1 - n) * SB_KB, SB_KB)
        kblk = kc_ref[0, 0, pl.ds(kk, SB_KB), :].astype(BF16)
        vblk = vc_ref[0, 0, pl.ds(kk, SB_KB), :].astype(BF16)
        do, dc = _sb_tile(q4, kblk, vblk, car[...], tri, None)
        acc[...] += do
        car[...] += dc
        return 0

    lax.fori_loop(0, n_tiles, step, 0)
    o_ref[...] = _rms(_unstack_heads(acc[...], lq), gn[...])


def _sb_sample(zd, row0, n_streams, lq, k_cache, v_cache, layer, gn):
    past = k_cache.shape[2]
    rows = HEADS * lq
    new = lambda cb: pl.BlockSpec((lq, WG), functools.partial(lambda s, cb: (row0 // lq + s, cb), cb=cb))
    cache = pl.BlockSpec((1, 1, past, WG), lambda s: (layer, s, 0, 0))
    return pl.pallas_call(
        functools.partial(_sbs_body, lq=lq, past=past),
        grid=(n_streams,),
        in_specs=[new(0), new(1), new(2), cache, cache, pl.BlockSpec((1, WG), lambda s: (0, 0))],
        out_specs=pl.BlockSpec((lq, WG), lambda s: (s, 0)),
        out_shape=jax.ShapeDtypeStruct((n_streams * lq, WG), F32),
        scratch_shapes=[pltpu.VMEM((rows, WG), F32), pltpu.VMEM((rows, 1), F32)],
        compiler_params=pltpu.CompilerParams(dimension_semantics=("parallel",), vmem_limit_bytes=VMEM_LIMIT),
        name="sb_sample",
    )(zd, zd, zd, k_cache, v_cache, gn)


def _w_in_permutation(gdn_in, ml_in, s5_in):
    perm = np.full((NZ,), -1, np.int64)
    a0, b0 = 0, gdn_in
    c0 = b0 + ml_in
    d0 = c0 + s5_in
    perm[Z_A_QKV:Z_A_QKV + 3 * WG] = a0 + np.arange(3 * WG)
    perm[Z_D_QKV:Z_D_QKV + 3 * WG] = d0 + np.arange(3 * WG)
    perm[Z_B_QKV:Z_B_QKV + 3 * WG] = b0 + np.arange(3 * WG)
    perm[Z_A_GOUT:Z_A_GOUT + WG] = a0 + 3 * WG + 2 * HEADS + np.arange(WG)
    perm[Z_B_OG:Z_B_OG + WG] = b0 + 3 * WG + 2 * HEADS + np.arange(WG)
    perm[Z_C:Z_C + WG] = c0 + np.arange(WG)
    perm[Z_S1 + ML_LANE:Z_S1 + ML_LANE + HEADS] = b0 + 3 * WG + np.arange(HEADS)
    perm[Z_S1 + GDN_LANE:Z_S1 + GDN_LANE + HEADS] = a0 + 3 * WG + np.arange(HEADS)
    perm[Z_S2 + ML_LANE:Z_S2 + ML_LANE + HEADS] = b0 + 3 * WG + HEADS + np.arange(HEADS)
    perm[Z_S2 + GDN_LANE:Z_S2 + GDN_LANE + HEADS] = a0 + 3 * WG + HEADS + np.arange(HEADS)
    return perm


def _lane_row(v, lane0, width=LANES):
    return jnp.zeros((1, width), F32).at[0, lane0:lane0 + v.shape[0]].set(v.astype(F32))


def _to_block_diag(s):
    out = jnp.zeros((s.shape[0], WG, WG), F32)
    for h in range(HEADS):
        out = out.at[:, h * HD:(h + 1) * HD, h * HD:(h + 1) * HD].set(s[:, h].astype(F32))
    return out


def _from_block_diag(s):
    return jnp.stack([s[:, h * HD:(h + 1) * HD, h * HD:(h + 1) * HD] for h in range(HEADS)], axis=1)


def _s5_block_diag_in(b):
    out = jnp.zeros((S5_GROUPS * S5_CH, S5_N), F32)
    for gi in range(S5_GROUPS):
        out = out.at[gi * S5_CH:(gi + 1) * S5_CH, gi * S5_P:(gi + 1) * S5_P].set(b[gi].T)
    return out


def _s5_block_diag_out(cm):
    out = jnp.zeros((S5_N, S5_GROUPS * S5_CH), F32)
    for gi in range(S5_GROUPS):
        out = out.at[gi * S5_P:(gi + 1) * S5_P, gi * S5_CH:(gi + 1) * S5_CH].set(cm[gi].T)
    return out


def kernel(x_prompt, x_sample, cache_sb_k, cache_sb_v, state_gdn_s, state_gdn_conv, state_mlstm_c, state_mlstm_n, state_mlstm_m, state_s5_re, state_s5_im, ffn1_norm, ffn1_w_gate, ffn1_w_up, ffn1_w_down, mix_norm, w_in, gdn_conv_w, gdn_a_log, gdn_dt_bias, gdn_norm, ml_b_i, ml_b_f, ml_norm, s5_a_re, s5_a_im, s5_b_re, s5_b_im, s5_c_re, s5_c_im, s5_d, s5_log_step, s5_w_glu, s5_b_glu, s5_norm, sb_norm, w_out, ffn2_norm, ffn2_w_gate, ffn2_w_up, ffn2_w_down, final_norm):
    depth = w_in.shape[0]
    n_p, len_p, d_model = x_prompt.shape
    n_s, len_s, _ = x_sample.shape
    past = cache_sb_k.shape[2]
    assert n_p == 1
    t_p, t_s = n_p * len_p, n_s * len_s
    c_p, c_s = min(64, len_p), min(64, len_s)
    g_p = 4 if len_p % (4 * c_p) == 0 else 1

    x = jnp.concatenate([x_prompt.reshape(t_p, d_model), x_sample.reshape(t_s, d_model)], axis=0)
    perm = _w_in_permutation(3 * WG + 2 * HEADS + WG, 3 * WG + 2 * HEADS + WG, WG)
    k_cache = cache_sb_k.reshape(depth, n_s, past, WG)
    v_cache = cache_sb_v.reshape(depth, n_s, past, WG)
    zeros = lambda *s: jnp.zeros(s, F32)
    row = lambda v: v.reshape(1, -1).astype(F32)
    outs_p = [[] for _ in range(9)]
    outs_s = [[] for _ in range(9)]

    for l in range(depth):
        x = _ffn(x, row(ffn1_norm[l]), ffn1_w_gate[l].astype(BF16), ffn1_w_up[l].astype(BF16),
                 ffn1_w_down[l].astype(BF16))
        w_p = jnp.where(perm[None, :] >= 0, w_in[l][:, np.maximum(perm, 0)], 0.0).astype(BF16)
        z, zd = _proj(x, row(mix_norm[l]), w_p)

        cw = jnp.concatenate([gdn_conv_w[l], zeros(8 - CONV_W, 3 * WG)], axis=0)
        alog, dtb = _lane_row(gdn_a_log[l], GDN_LANE), _lane_row(gdn_dt_bias[l], GDN_LANE)
        gn_a = row(jnp.tile(gdn_norm[l], HEADS))
        conv_s = jnp.concatenate([zeros(n_s, 8 - (CONV_W - 1), 3 * WG), state_gdn_conv[l]], axis=1)
        oa_p, sa_p = _gdn(z, 0, n_p, len_p, c_p, g_p, zeros(n_p, 8, 3 * WG), zeros(n_p, WG, WG), cw, alog, dtb, gn_a)
        oa_s, sa_s = _gdn(z, t_p, n_s, len_s, c_s, 1, conv_s, _to_block_diag(state_gdn_s[l]), cw, alog, dtb, gn_a)

        b_i, b_f = _lane_row(ml_b_i[l], ML_LANE), _lane_row(ml_b_f[l], ML_LANE)
        gn_b = row(ml_norm[l])
        ob_p, cb_p, nb_p, mb_p = _mlstm(z, 0, n_p, len_p, c_p, g_p, zeros(n_p, WG, WG), zeros(n_p, 1, WG),
                                        zeros(n_p, 1, LANES), b_i, b_f, gn_b)
        m0_s = jnp.zeros((n_s, 1, LANES), F32).at[:, 0, ML_LANE:ML_LANE + HEADS].set(state_mlstm_m[l])
        ob_s, cb_s, nb_s, mb_s = _mlstm(z, t_p, n_s, len_s, c_s, 1, _to_block_diag(state_mlstm_c[l]),
                                        state_mlstm_n[l].reshape(n_s, 1, WG), m0_s, b_i, b_f, gn_b)

        s5_consts = (row(s5_a_re[l]), row(s5_a_im[l]),
                     row(jnp.broadcast_to(jnp.exp(s5_log_step[l])[:, None], (S5_GROUPS, S5_P))),
                     _s5_block_diag_in(s5_b_re[l]).astype(BF16), _s5_block_diag_in(s5_b_im[l]).astype(BF16),
                     _s5_block_diag_out(s5_c_re[l]).astype(BF16), _s5_block_diag_out(s5_c_im[l]).astype(BF16),
                     row(s5_d[l]), s5_w_glu[l].astype(BF16), row(s5_b_glu[l]), row(s5_norm[l]))
        oc_p, xr_p, xi_p = _s5(z, 0, n_p, len_p, min(128, len_p), zeros(n_p, 1, S5_N), zeros(n_p, 1, S5_N),
                               *s5_consts)
        oc_s, xr_s, xi_s = _s5(z, t_p, n_s, len_s, len_s, state_s5_re[l].reshape(n_s, 1, S5_N),
                               state_s5_im[l].reshape(n_s, 1, S5_N), *s5_consts)

        gn_d = row(sb_norm[l])
        od_p = _sb_prompt(zd, len_p, gn_d)
        od_s = _sb_sample(zd, t_p, n_s, len_s, k_cache, v_cache, l, gn_d)

        mix = tuple(jnp.concatenate([p, s], axis=0) for p, s in
                    ((oa_p, oa_s), (ob_p, ob_s), (oc_p, oc_s), (od_p, od_s))) + (w_out[l].astype(BF16),)
        x = _ffn(x, row(ffn2_norm[l]), ffn2_w_gate[l].astype(BF16), ffn2_w_up[l].astype(BF16),
                 ffn2_w_down[l].astype(BF16), mix=mix, final_g=row(final_norm) if l == depth - 1 else None)

        k_new = z[:, Z_D_QKV + WG:Z_D_QKV + 2 * WG]
        v_new = z[:, Z_D_QKV + 2 * WG:Z_D_QKV + 3 * WG]
        z_conv = z[:, Z_A_QKV:Z_A_QKV + 3 * WG]
        tail = CONV_W - 1
        for outs, r0, n, ln, sa, cb, nb, mb, xr, xi in (
                (outs_p, 0, n_p, len_p, sa_p, cb_p, nb_p, mb_p, xr_p, xi_p),
                (outs_s, t_p, n_s, len_s, sa_s, cb_s, nb_s, mb_s, xr_s, xi_s)):
            rows = slice(r0, r0 + n * ln)
            outs[0].append(k_new[rows].reshape(n, ln, HEADS, HD))
            outs[1].append(v_new[rows].reshape(n, ln, HEADS, HD))
            outs[2].append(_from_block_diag(sa))
            outs[3].append(z_conv[rows].reshape(n, ln, 3 * WG)[:, ln - tail:, :])
            outs[4].append(_from_block_diag(cb))
            outs[5].append(nb.reshape(n, HEADS, HD))
            outs[6].append(mb[:, 0, ML_LANE:ML_LANE + HEADS])
            outs[7].append(xr.reshape(n, S5_GROUPS, S5_P))
            outs[8].append(xi.reshape(n, S5_GROUPS, S5_P))

    y_prompt = x[:t_p].reshape(n_p, len_p, d_model)
    y_sample = x[t_p:].reshape(n_s, len_s, d_model)
    return (y_prompt, y_sample, *[jnp.stack(o) for o in outs_p], *[jnp.stack(o) for o in outs_s])
```

```python
import functools

import numpy as np
import jax
import jax.numpy as jnp
from jax import lax
from jax.experimental import pallas as pl
from jax.experimental.pallas import tpu as pltpu

F32 = jnp.float32
BF16 = jnp.bfloat16
HI = lax.Precision.HIGHEST
EPS = 1e-6

HEADS = 4
HD = 64
WG = HEADS * HD
LANES = 128
CONV_W = 4
S5_GROUPS, S5_CH, S5_P = 16, 16, 64
S5_N = S5_GROUPS * S5_P
NEG_INF = float("-inf")

Z_A_QKV, Z_D_QKV, Z_B_QKV = 0, 768, 1536
Z_A_GOUT, Z_B_OG, Z_C = 2304, 2560, 2816
Z_S1, Z_S2 = 3072, 3200
NZ = 3328
ML_LANE, GDN_LANE = 0, 4

VMEM_LIMIT = 56 * 1024 * 1024


def _dot(a, b, precision=None):
    return jnp.dot(a, b, preferred_element_type=F32, precision=precision)


def _dot_nt(a, b, precision=None):
    return lax.dot_general(a, b, (((1,), (1,)), ((), ())), preferred_element_type=F32, precision=precision)


def _dot_tn(a, b, precision=None):
    return lax.dot_general(a, b, (((0,), (0,)), ((), ())), preferred_element_type=F32, precision=precision)


def _rms(x, g):
    return x * lax.rsqrt(jnp.mean(x * x, axis=-1, keepdims=True) + EPS) * g


def _head_masks(dtype=F32):
    lane = lax.broadcasted_iota(jnp.int32, (1, WG), 1)
    return [((lane >= h * HD) & (lane < (h + 1) * HD)).astype(dtype) for h in range(HEADS)]


def _block_diag_mask():
    shift = HD.bit_length() - 1
    r = jnp.right_shift(lax.broadcasted_iota(jnp.int32, (WG, WG), 0), shift)
    c = jnp.right_shift(lax.broadcasted_iota(jnp.int32, (WG, WG), 1), shift)
    return (r == c).astype(F32)


def _tri(c, strict):
    r = lax.broadcasted_iota(jnp.int32, (c, c), 0)
    col = lax.broadcasted_iota(jnp.int32, (c, c), 1)
    return (r > col) if strict else (r >= col)


def _ffn_body(*refs, n_f, has_mix, has_final):
    it = iter(refs)
    x_ref = next(it)
    if has_mix:
        o_refs = [next(it) for _ in range(4)]
        wo_ref = next(it)
    g_ref, wg_ref, wu_ref, wd_ref = next(it), next(it), next(it), next(it)
    gf_ref = next(it) if has_final else None
    out_ref = next(it)
    xin_scr, h_scr, acc_scr = next(it), next(it), next(it)
    j = pl.program_id(1)

    @pl.when(j == 0)
    def _():
        x = x_ref[...]
        if has_mix:
            for m, o_ref in enumerate(o_refs):
                x = x + _dot(o_ref[...].astype(BF16), wo_ref[m * WG:(m + 1) * WG, :])
        xin_scr[...] = x
        h_scr[...] = _rms(x, g_ref[...]).astype(BF16)
        acc_scr[...] = jnp.zeros_like(acc_scr)

    h = h_scr[...]
    a = _dot(h, wg_ref[...])
    b = _dot(h, wu_ref[...])
    t = (a * jax.nn.sigmoid(a)) * b
    acc_scr[...] += _dot(t.astype(BF16), wd_ref[...])

    @pl.when(j == n_f - 1)
    def _():
        y = xin_scr[...] + 0.5 * acc_scr[...]
        if has_final:
            y = _rms(y, gf_ref[...])
        out_ref[...] = y


def _ffn(x, g, wg, wu, wd, mix=None, final_g=None, *, tm=512, tf=512):
    t, d = x.shape
    f = wg.shape[1]
    n_f = f // tf
    row = lambda i, j: (i, 0)
    const = lambda i, j: (0, 0)
    in_specs = [pl.BlockSpec((tm, d), row)]
    args = [x]
    if mix is not None:
        in_specs += [pl.BlockSpec((tm, WG), row)] * 4 + [pl.BlockSpec(mix[4].shape, const)]
        args += list(mix)
    in_specs += [pl.BlockSpec((1, d), const), pl.BlockSpec((d, tf), lambda i, j: (0, j)),
                 pl.BlockSpec((d, tf), lambda i, j: (0, j)), pl.BlockSpec((tf, d), lambda i, j: (j, 0))]
    args += [g, wg, wu, wd]
    if final_g is not None:
        in_specs.append(pl.BlockSpec((1, d), const))
        args.append(final_g)
    return pl.pallas_call(
        functools.partial(_ffn_body, n_f=n_f, has_mix=mix is not None, has_final=final_g is not None),
        grid=(t // tm, n_f),
        in_specs=in_specs,
        out_specs=pl.BlockSpec((tm, d), row),
        out_shape=jax.ShapeDtypeStruct((t, d), F32),
        scratch_shapes=[pltpu.VMEM((tm, d), F32), pltpu.VMEM((tm, d), BF16), pltpu.VMEM((tm, d), F32)],
        compiler_params=pltpu.CompilerParams(dimension_semantics=("parallel", "arbitrary"),
                                             vmem_limit_bytes=VMEM_LIMIT),
        name="ffn",
    )(*args)


def _proj_body(x_ref, g_ref, w_ref, z_ref, zd_ref, h_scr):
    j = pl.program_id(1)

    @pl.when(j == 0)
    def _():
        h_scr[...] = _rms(x_ref[...], g_ref[...]).astype(BF16)

    z = _dot(h_scr[...], w_ref[...])
    z_ref[...] = z

    @pl.when(j == 0)
    def _():
        zd_ref[...] = z[:, Z_D_QKV:Z_D_QKV + 3 * WG].astype(BF16)


def _proj(x, g, w, *, tm=512, tn=NZ // 2):
    t, d = x.shape
    assert Z_D_QKV + 3 * WG <= tn
    return pl.pallas_call(
        _proj_body,
        grid=(t // tm, NZ // tn),
        in_specs=[pl.BlockSpec((tm, d), lambda i, j: (i, 0)), pl.BlockSpec((1, d), lambda i, j: (0, 0)),
                  pl.BlockSpec((d, tn), lambda i, j: (0, j))],
        out_specs=[pl.BlockSpec((tm, tn), lambda i, j: (i, j)), pl.BlockSpec((tm, 3 * WG), lambda i, j: (i, 0))],
        out_shape=[jax.ShapeDtypeStruct((t, NZ), F32), jax.ShapeDtypeStruct((t, 3 * WG), BF16)],
        scratch_shapes=[pltpu.VMEM((tm, d), BF16)],
        compiler_params=pltpu.CompilerParams(dimension_semantics=("parallel", "arbitrary"),
                                             vmem_limit_bytes=VMEM_LIMIT),
        name="proj",
    )(x, g, w)


INV_BASE = 8


def _bdot(a, b):
    return _dot(a.astype(BF16), b.astype(BF16))


def _bdot_nt(a, b):
    return _dot_nt(a.astype(BF16), b.astype(BF16))


def _bdot_tn(a, b):
    return _dot_tn(a.astype(BF16), b.astype(BF16))


def _unit_lower_inverse(m, n, top):
    r = lax.broadcasted_iota(jnp.int32, (n, n), 0)
    col = lax.broadcasted_iota(jnp.int32, (n, n), 1)
    same = lambda s: jnp.right_shift(r, s.bit_length() - 1) == jnp.right_shift(col, s.bit_length() - 1)
    p = jnp.where(same(INV_BASE), -m, 0.0)
    x = (r == col).astype(F32) + p
    k = 2
    while k < INV_BASE:
        p = _bdot(p, p)
        x = x + _bdot(x, p)
        k *= 2
    s = INV_BASE
    while s < top:
        join = jnp.where(same(2 * s) & jnp.logical_not(same(s)), m, 0.0)
        x = x - _bdot(x, _bdot(join, x))
        s *= 2
    return x


def _stack_rows(x, masks):
    return jnp.concatenate([x * mk for mk in masks], axis=0)


def _gdn_chunk(qkv, s1, s2, s_bd, alog, dtb, hm, bd, c):
    n = HEADS * c
    q, k, v = qkv[:, 0:WG], qkv[:, WG:2 * WG], qkv[:, 2 * WG:3 * WG]
    q = q * lax.rsqrt(_dot(q * q, bd, HI) + EPS) * (HD ** -0.5)
    k = k * lax.rsqrt(_dot(k * k, bd, HI) + EPS)
    log_a = -jnp.exp(alog) * jax.nn.softplus(s1 + dtb)
    beta = jax.nn.sigmoid(s2)
    lane = lax.broadcasted_iota(jnp.int32, (1, LANES), 1)
    gate_lane = [(lane == GDN_LANE + h).astype(F32) for h in range(HEADS)]
    r = lax.broadcasted_iota(jnp.int32, (n, n), 0)
    col = lax.broadcasted_iota(jnp.int32, (n, n), 1)
    shift = c.bit_length() - 1
    same_head = jnp.right_shift(r, shift) == jnp.right_shift(col, shift)
    incl, strict = same_head & (r >= col), same_head & (r > col)
    g4 = _dot(incl.astype(F32), _stack_rows(log_a, gate_lane), HI)
    gcol = jnp.sum(g4, axis=-1, keepdims=True)
    grow = _dot_nt(jnp.ones((8, LANES), F32), g4, HI)[0:1, :]
    bcol = jnp.sum(_stack_rows(beta, gate_lane), axis=-1, keepdims=True)
    decay = jnp.exp(jnp.where(incl, gcol - grow, NEG_INF))
    k4, q4, v4 = _stack_rows(k, hm), _stack_rows(q, hm), _stack_rows(v, hm)
    m = jnp.where(strict, bcol * _bdot_nt(k4, k4) * decay, 0.0)
    t = _unit_lower_inverse(m, n, c)
    gam = jnp.exp(gcol)
    wv = _bdot(t, bcol * v4)
    wk = _bdot(t, (bcol * gam) * k4)
    p = _bdot_nt(q4, k4) * decay
    rowblk = jnp.right_shift(lax.broadcasted_iota(jnp.int32, (n, 1), 0), shift)
    glast = jnp.zeros((n, 1), F32)
    gl_lanes = jnp.zeros((1, WG), F32)
    for h in range(HEADS):
        g_h = gcol[(h + 1) * c - 1:(h + 1) * c, :]
        glast = glast + jnp.where(rowblk == h, g_h, 0.0)
        gl_lanes = gl_lanes + hm[h] * jnp.exp(g_h)
    u = wv - _bdot(wk, s_bd)
    o4 = _bdot(q4 * gam, s_bd) + _bdot(p, u)
    s_bd = s_bd * gl_lanes + _bdot_tn(k4 * jnp.exp(glast - gcol), u)
    o = o4[0:c, :]
    for h in range(1, HEADS):
        o = o + o4[h * c:(h + 1) * c, :]
    return o, s_bd


def _gdn_body(zqkv, zgo, s1r, s2r, conv8, s0, cw, alog, dtb, gn, o_ref, sfin_ref, xbuf, s_scr, *, c, g, n_steps):
    i = pl.program_id(1)
    r = c * g

    @pl.when(i == 0)
    def _():
        xbuf[0:8, :] = conv8[0]
        s_scr[...] = s0[0]

    xbuf[8:8 + r, :] = zqkv[...]
    w = cw[...]
    y = (xbuf[8:8 + r, :] * w[3:4, :] + xbuf[7:7 + r, :] * w[2:3, :]
         + xbuf[6:6 + r, :] * w[1:2, :] + xbuf[5:5 + r, :] * w[0:1, :])
    xbuf[0:8, :] = xbuf[r:r + 8, :]
    qkv = y * jax.nn.sigmoid(y)
    hm, bd = _head_masks(), _block_diag_mask()
    s_bd = s_scr[...]
    for ci in range(g):
        rows = slice(ci * c, (ci + 1) * c)
        o, s_bd = _gdn_chunk(qkv[rows], s1r[rows, :], s2r[rows, :], s_bd, alog[...], dtb[...], hm, bd, c)
        on = o * lax.rsqrt(_dot(o * o, bd, HI) * (1.0 / HD) + EPS) * gn[...]
        gate = zgo[rows, :]
        o_ref[rows, :] = on * (gate * jax.nn.sigmoid(gate))
    s_scr[...] = s_bd

    @pl.when(i == n_steps - 1)
    def _():
        sfin_ref[0] = s_bd


def _stream_specs(row0, r, n_steps, cols):
    return [pl.BlockSpec((r, w), functools.partial(lambda s, i, cb: (row0 // r + s * n_steps + i, cb), cb=off // w))
            for off, w in cols]


def _per_stream(shape):
    nd = len(shape)
    return pl.BlockSpec((1,) + tuple(shape[1:]), lambda s, i: (s,) + (0,) * (nd - 1))


def _const2(shape):
    return pl.BlockSpec(shape, lambda s, i: (0, 0))


def _gdn(z, row0, n_streams, length, c, g, conv8, s0, cw, alog, dtb, gn):
    r = c * g
    n_steps = length // r
    in_specs = _stream_specs(row0, r, n_steps, [(Z_A_QKV, 3 * WG), (Z_A_GOUT, WG), (Z_S1, LANES), (Z_S2, LANES)])
    in_specs += [_per_stream(conv8.shape), _per_stream(s0.shape),
                 _const2(cw.shape), _const2(alog.shape), _const2(dtb.shape), _const2(gn.shape)]
    return pl.pallas_call(
        functools.partial(_gdn_body, c=c, g=g, n_steps=n_steps),
        grid=(n_streams, n_steps),
        in_specs=in_specs,
        out_specs=[pl.BlockSpec((r, WG), lambda s, i: (s * n_steps + i, 0)), _per_stream(s0.shape)],
        out_shape=[jax.ShapeDtypeStruct((n_streams * length, WG), F32), jax.ShapeDtypeStruct(s0.shape, F32)],
        scratch_shapes=[pltpu.VMEM((r + 8, 3 * WG), F32), pltpu.VMEM((WG, WG), F32)],
        compiler_params=pltpu.CompilerParams(dimension_semantics=("parallel", "arbitrary"),
                                             vmem_limit_bytes=VMEM_LIMIT),
        name="gdn",
    )(z, z, z, z, conv8, s0, cw, alog, dtb, gn)


def _mlstm_chunk(qkv, s1, s2, c_bd, n_row, m_vec, b_i, b_f, hm, bd, c):
    n = HEADS * c
    q, k, v = qkv[:, 0:WG], qkv[:, WG:2 * WG] * (HD ** -0.5), qkv[:, 2 * WG:3 * WG]
    i_pre = s1 + b_i
    log_f = jax.nn.log_sigmoid(s2 + b_f)
    lane = lax.broadcasted_iota(jnp.int32, (1, LANES), 1)
    gate_lane = [(lane == ML_LANE + h).astype(F32) for h in range(HEADS)]
    r = lax.broadcasted_iota(jnp.int32, (n, n), 0)
    col = lax.broadcasted_iota(jnp.int32, (n, n), 1)
    shift = c.bit_length() - 1
    incl = (jnp.right_shift(r, shift) == jnp.right_shift(col, shift)) & (r >= col)
    i4 = _stack_rows(i_pre, gate_lane)
    b4 = _dot(incl.astype(F32), _stack_rows(log_f, gate_lane), HI)
    bcol = jnp.sum(b4, axis=-1, keepdims=True)
    icol = jnp.sum(i4, axis=-1, keepdims=True)
    drow = _dot_nt(jnp.ones((8, LANES), F32), i4 - b4, HI)[0:1, :]
    dmat = jnp.where(incl, bcol + drow, NEG_INF)
    dmax = jnp.max(dmat, axis=-1, keepdims=True)
    rowblk = jnp.right_shift(lax.broadcasted_iota(jnp.int32, (n, 1), 0), shift)
    m_old = jnp.zeros((n, 1), F32)
    for h in range(HEADS):
        m_old = m_old + jnp.where(rowblk == h, m_vec[:, ML_LANE + h:ML_LANE + h + 1], 0.0)
    inter = bcol + m_old
    mt = jnp.maximum(inter, dmax)
    w_int = jnp.exp(inter - mt)
    q4, k4, v4 = _stack_rows(q, hm), _stack_rows(k, hm), _stack_rows(v, hm)
    wqk = jnp.exp(dmat - mt) * _bdot_nt(q4, k4)
    num = w_int * _bdot(q4, c_bd) + _bdot(wqk, v4)
    den = w_int * jnp.sum(q4 * n_row, axis=-1, keepdims=True) + jnp.sum(wqk, axis=-1, keepdims=True)
    h4 = num / jnp.maximum(jnp.abs(den), jnp.exp(-mt))
    hh = h4[0:c, :]
    for h in range(1, HEADS):
        hh = hh + h4[h * c:(h + 1) * c, :]
    m_new = jnp.zeros((n, 1), F32)
    blast = jnp.zeros((n, 1), F32)
    f_lanes = jnp.zeros((1, WG), F32)
    m_next = jnp.zeros((1, LANES), F32)
    for h in range(HEADS):
        last = slice((h + 1) * c - 1, (h + 1) * c)
        m_h, b_h = mt[last, :], bcol[last, :]
        m_new = m_new + jnp.where(rowblk == h, m_h, 0.0)
        blast = blast + jnp.where(rowblk == h, b_h, 0.0)
        f_lanes = f_lanes + hm[h] * jnp.exp(b_h + m_vec[:, ML_LANE + h:ML_LANE + h + 1] - m_h)
        m_next = m_next + jnp.where(lane == ML_LANE + h, m_h, 0.0)
    kw = k4 * jnp.exp(blast - bcol + icol - m_new)
    c_bd = c_bd * f_lanes + _bdot_tn(kw, v4)
    n_row = n_row * f_lanes + jnp.sum(kw, axis=0, keepdims=True)
    return hh, c_bd, n_row, m_next


def _mlstm_body(zqkv, zog, s1r, s2r, c0, n0, m0, b_i, b_f, gn, o_ref, cfin, nfin, mfin, c_scr, n_scr, m_scr,
                *, c, g, n_steps):
    i = pl.program_id(1)

    @pl.when(i == 0)
    def _():
        c_scr[...] = c0[0]
        n_scr[...] = n0[0]
        m_scr[...] = m0[0]

    hm, bd = _head_masks(), _block_diag_mask()
    c_bd, n_row, m_vec = c_scr[...], n_scr[...], m_scr[...]
    for ci in range(g):
        rows = slice(ci * c, (ci + 1) * c)
        hh, c_bd, n_row, m_vec = _mlstm_chunk(zqkv[rows, :], s1r[rows, :], s2r[rows, :], c_bd, n_row, m_vec,
                                              b_i[...], b_f[...], hm, bd, c)
        hn = hh * lax.rsqrt(_dot(hh * hh, bd, HI) * (1.0 / HD) + EPS) * gn[...]
        o_ref[rows, :] = jax.nn.sigmoid(zog[rows, :]) * hn
    c_scr[...] = c_bd
    n_scr[...] = n_row
    m_scr[...] = m_vec

    @pl.when(i == n_steps - 1)
    def _():
        cfin[0] = c_bd
        nfin[0] = n_row
        mfin[0] = m_vec


def _mlstm(z, row0, n_streams, length, c, g, c0, n0, m0, b_i, b_f, gn):
    r = c * g
    n_steps = length // r
    in_specs = _stream_specs(row0, r, n_steps, [(Z_B_QKV, 3 * WG), (Z_B_OG, WG), (Z_S1, LANES), (Z_S2, LANES)])
    in_specs += [_per_stream(c0.shape), _per_stream(n0.shape), _per_stream(m0.shape),
                 _const2(b_i.shape), _const2(b_f.shape), _const2(gn.shape)]
    return pl.pallas_call(
        functools.partial(_mlstm_body, c=c, g=g, n_steps=n_steps),
        grid=(n_streams, n_steps),
        in_specs=in_specs,
        out_specs=[pl.BlockSpec((r, WG), lambda s, i: (s * n_steps + i, 0)),
                   _per_stream(c0.shape), _per_stream(n0.shape), _per_stream(m0.shape)],
        out_shape=[jax.ShapeDtypeStruct((n_streams * length, WG), F32), jax.ShapeDtypeStruct(c0.shape, F32),
                   jax.ShapeDtypeStruct(n0.shape, F32), jax.ShapeDtypeStruct(m0.shape, F32)],
        scratch_shapes=[pltpu.VMEM((WG, WG), F32), pltpu.VMEM((1, WG), F32), pltpu.VMEM((1, LANES), F32)],
        compiler_params=pltpu.CompilerParams(dimension_semantics=("parallel", "arbitrary"),
                                             vmem_limit_bytes=VMEM_LIMIT),
        name="mlstm",
    )(z, z, z, z, c0, n0, m0, b_i, b_f, gn)


def _s5_body(zc, x0r, x0i, a_re, a_im, step, b_re, b_im, c_re, c_im, d_row, wglu, bglu, gn,
             o_ref, xfr, xfi, par, st, xs_r, xs_i, *, r, n_steps):
    i = pl.program_id(1)

    @pl.when(i == 0)
    def _():
        ar, ai = a_re[...], a_im[...]
        mag = jnp.exp(step[...] * ar)
        abr, abi = mag * jnp.cos(step[...] * ai), mag * jnp.sin(step[...] * ai)
        den = ar * ar + ai * ai
        nr, ni = abr - 1.0, abi
        par[0:1, :] = abr
        par[1:2, :] = abi
        par[2:3, :] = (nr * ar + ni * ai) / den
        par[3:4, :] = (ni * ar - nr * ai) / den
        st[0:1, :] = x0r[0]
        st[1:2, :] = x0i[0]

    abr, abi, f_re, f_im = par[0:1, :], par[1:2, :], par[2:3, :], par[3:4, :]
    u = zc[...]
    ub = u.astype(BF16)
    pr, pi = _dot(ub, b_re[...]), _dot(ub, b_im[...])
    bu_r = f_re * pr - f_im * pi
    bu_i = f_re * pi + f_im * pr
    xr, xi = st[0:1, :], st[1:2, :]
    for t in range(r):
        xr, xi = abr * xr - abi * xi + bu_r[t:t + 1, :], abr * xi + abi * xr + bu_i[t:t + 1, :]
        xs_r[t:t + 1, :] = xr
        xs_i[t:t + 1, :] = xi
    st[0:1, :] = xr
    st[1:2, :] = xi
    y = _dot(xs_r[...].astype(BF16), c_re[...]) - _dot(xs_i[...].astype(BF16), c_im[...]) + d_row[...] * u
    y = jax.nn.gelu(y)
    gate = jax.nn.sigmoid(_dot(y.astype(BF16), wglu[...]) + bglu[...])
    o_ref[...] = _rms(y * gate, gn[...])

    @pl.when(i == n_steps - 1)
    def _():
        xfr[0] = xr
        xfi[0] = xi


def _s5(z, row0, n_streams, length, r, x0r, x0i, a_re, a_im, step, b_re, b_im, c_re, c_im, d_row, wglu, bglu, gn):
    n_steps = length // r
    in_specs = _stream_specs(row0, r, n_steps, [(Z_C, WG)])
    in_specs += [_per_stream(x0r.shape), _per_stream(x0i.shape)]
    consts = [a_re, a_im, step, b_re, b_im, c_re, c_im, d_row, wglu, bglu, gn]
    in_specs += [_const2(a.shape) for a in consts]
    return pl.pallas_call(
        functools.partial(_s5_body, r=r, n_steps=n_steps),
        grid=(n_streams, n_steps),
        in_specs=in_specs,
        out_specs=[pl.BlockSpec((r, WG), lambda s, i: (s * n_steps + i, 0)),
                   _per_stream(x0r.shape), _per_stream(x0i.shape)],
        out_shape=[jax.ShapeDtypeStruct((n_streams * length, WG), F32), jax.ShapeDtypeStruct(x0r.shape, F32),
                   jax.ShapeDtypeStruct(x0i.shape, F32)],
        scratch_shapes=[pltpu.VMEM((8, S5_N), F32), pltpu.VMEM((8, S5_N), F32),
                        pltpu.VMEM((r, S5_N), F32), pltpu.VMEM((r, S5_N), F32)],
        compiler_params=pltpu.CompilerParams(dimension_semantics=("parallel", "arbitrary"),
                                             vmem_limit_bytes=VMEM_LIMIT),
        name="s5",
    )(z, x0r, x0i, *consts)


SB_KB = 256
SB_ZERO_BELOW = -110.0


def _sb_live(carry):
    return jnp.max(carry) >= SB_ZERO_BELOW


def _stack_heads(q):
    lane = lax.broadcasted_iota(jnp.int32, (1, WG), 1)
    return jnp.concatenate([jnp.where((lane >= h * HD) & (lane < (h + 1) * HD), q, jnp.zeros_like(q))
                            for h in range(HEADS)], axis=0)


def _unstack_heads(o4, n):
    hm = _head_masks()
    return sum(hm[h] * o4[h * n:(h + 1) * n, :] for h in range(HEADS))


def _sb_tile(q4, kblk, vblk, carry, tri, valid):
    z = _dot_nt(q4, kblk) * (HD ** -0.5)
    ls = jnp.minimum(z, 0.0) - jnp.log1p(jnp.exp(-jnp.abs(z)))
    ln = ls - z
    if valid is not None:
        ln = jnp.where(valid, ln, 0.0)
    after = _dot(ln.astype(BF16), tri) + carry
    a = jnp.exp(ls + after)
    if valid is not None:
        a = jnp.where(valid, a, 0.0)
    return _dot(a.astype(BF16), vblk), jnp.sum(ln, axis=-1, keepdims=True)


def _suffix_tri():
    j = lax.broadcasted_iota(jnp.int32, (SB_KB, SB_KB), 0)
    s = lax.broadcasted_iota(jnp.int32, (SB_KB, SB_KB), 1)
    return (j > s).astype(BF16)


def _sbp_body(q_ref, k_ref, v_ref, gn, o_ref, acc, car, *, qb):
    i = pl.program_id(0)
    q4 = _stack_heads(q_ref[...])
    rows = HEADS * qb
    qpos = i * qb + (lax.broadcasted_iota(jnp.int32, (rows, 1), 0) & (qb - 1))
    tri = _suffix_tri()
    top = (i * qb + qb - 2) // SB_KB
    k0 = pl.multiple_of(top * SB_KB, SB_KB)
    kpos = k0 + lax.broadcasted_iota(jnp.int32, (1, SB_KB), 1)
    do, dc = _sb_tile(q4, k_ref[pl.ds(k0, SB_KB), :], v_ref[pl.ds(k0, SB_KB), :], 0.0, tri, kpos < qpos)
    acc[...] = do
    car[...] = dc

    def step(state):
        n, _ = state
        kk = pl.multiple_of((top - 1 - n) * SB_KB, SB_KB)
        do, dc = _sb_tile(q4, k_ref[pl.ds(kk, SB_KB), :], v_ref[pl.ds(kk, SB_KB), :], car[...], tri, None)
        acc[...] += do
        carry = car[...] + dc
        car[...] = carry
        return n + 1, _sb_live(carry)

    lax.while_loop(lambda st: (st[0] < top) & st[1], step, (jnp.int32(0), _sb_live(dc)))
    o_ref[...] = _rms(_unstack_heads(acc[...], qb), gn[...])


def _sb_prompt(zd, length, gn, *, qb=128):
    rows = HEADS * qb
    return pl.pallas_call(
        functools.partial(_sbp_body, qb=qb),
        grid=(length // qb,),
        in_specs=[pl.BlockSpec((qb, WG), lambda i: (i, 0)), pl.BlockSpec((length, WG), lambda i: (0, 1)),
                  pl.BlockSpec((length, WG), lambda i: (0, 2)), pl.BlockSpec((1, WG), lambda i: (0, 0))],
        out_specs=pl.BlockSpec((qb, WG), lambda i: (i, 0)),
        out_shape=jax.ShapeDtypeStruct((length, WG), F32),
        scratch_shapes=[pltpu.VMEM((rows, WG), F32), pltpu.VMEM((rows, 1), F32)],
        compiler_params=pltpu.CompilerParams(dimension_semantics=("arbitrary",), vmem_limit_bytes=VMEM_LIMIT),
        name="sb_prompt",
    )(zd, zd, zd, gn)


def _sbs_body(q_ref, kn_ref, vn_ref, kc_ref, vc_ref, gn, o_ref, acc, car, *, lq, past):
    q4 = _stack_heads(q_ref[...])
    rows = HEADS * lq
    qpos = past + (lax.broadcasted_iota(jnp.int32, (rows, 1), 0) & (lq - 1))
    tri = _suffix_tri()
    pad = jnp.zeros((SB_KB - lq, WG), BF16)
    kpos = past + lax.broadcasted_iota(jnp.int32, (1, SB_KB), 1)
    do, dc = _sb_tile(q4, jnp.concatenate([kn_ref[...], pad], axis=0), jnp.concatenate([vn_ref[...], pad], axis=0),
                      0.0, tri, kpos < qpos)
    acc[...] = do
    car[...] = dc
    n_tiles = past // SB_KB

    def step(state):
        n, _ = state
        kk = pl.multiple_of((n_tiles - 1 - n) * SB_KB, SB_KB)
        kblk = kc_ref[0, 0, pl.ds(kk, SB_KB), :].astype(BF16)
        vblk = vc_ref[0, 0, pl.ds(kk, SB_KB), :].astype(BF16)
        do, dc = _sb_tile(q4, kblk, vblk, car[...], tri, None)
        acc[...] += do
        carry = car[...] + dc
        car[...] = carry
        return n + 1, _sb_live(carry)

    lax.while_loop(lambda st: (st[0] < n_tiles) & st[1], step, (jnp.int32(0), _sb_live(dc)))
    o_ref[...] = _rms(_unstack_heads(acc[...], lq), gn[...])


def _sb_sample(zd, row0, n_streams, lq, k_cache, v_cache, layer, gn):
    past = k_cache.shape[2]
    rows = HEADS * lq
    new = lambda cb: pl.BlockSpec((lq, WG), functools.partial(lambda s, cb: (row0 // lq + s, cb), cb=cb))
    cache = pl.BlockSpec((1, 1, past, WG), lambda s: (layer, s, 0, 0))
    return pl.pallas_call(
        functools.partial(_sbs_body, lq=lq, past=past),
        grid=(n_streams,),
        in_specs=[new(0), new(1), new(2), cache, cache, pl.BlockSpec((1, WG), lambda s: (0, 0))],
        out_specs=pl.BlockSpec((lq, WG), lambda s: (s, 0)),
        out_shape=jax.ShapeDtypeStruct((n_streams * lq, WG), F32),
        scratch_shapes=[pltpu.VMEM((rows, WG), F32), pltpu.VMEM((rows, 1), F32)],
        compiler_params=pltpu.CompilerParams(dimension_semantics=("parallel",), vmem_limit_bytes=VMEM_LIMIT),
        name="sb_sample",
    )(zd, zd, zd, k_cache, v_cache, gn)


def _w_in_permutation(gdn_in, ml_in, s5_in):
    perm = np.full((NZ,), -1, np.int64)
    a0, b0 = 0, gdn_in
    c0 = b0 + ml_in
    d0 = c0 + s5_in
    perm[Z_A_QKV:Z_A_QKV + 3 * WG] = a0 + np.arange(3 * WG)
    perm[Z_D_QKV:Z_D_QKV + 3 * WG] = d0 + np.arange(3 * WG)
    perm[Z_B_QKV:Z_B_QKV + 3 * WG] = b0 + np.arange(3 * WG)
    perm[Z_A_GOUT:Z_A_GOUT + WG] = a0 + 3 * WG + 2 * HEADS + np.arange(WG)
    perm[Z_B_OG:Z_B_OG + WG] = b0 + 3 * WG + 2 * HEADS + np.arange(WG)
    perm[Z_C:Z_C + WG] = c0 + np.arange(WG)
    perm[Z_S1 + ML_LANE:Z_S1 + ML_LANE + HEADS] = b0 + 3 * WG + np.arange(HEADS)
    perm[Z_S1 + GDN_LANE:Z_S1 + GDN_LANE + HEADS] = a0 + 3 * WG + np.arange(HEADS)
    perm[Z_S2 + ML_LANE:Z_S2 + ML_LANE + HEADS] = b0 + 3 * WG + HEADS + np.arange(HEADS)
    perm[Z_S2 + GDN_LANE:Z_S2 + GDN_LANE + HEADS] = a0 + 3 * WG + HEADS + np.arange(HEADS)
    return perm


def _lane_row(v, lane0, width=LANES):
    return jnp.zeros((1, width), F32).at[0, lane0:lane0 + v.shape[0]].set(v.astype(F32))


def _to_block_diag(s):
    eye = jnp.eye(HEADS, dtype=F32)[None, :, None, :, None]
    return (s.astype(F32)[:, :, :, None, :] * eye).reshape(s.shape[0], WG, WG)


def _from_block_diag(s):
    return jnp.stack([s[:, h * HD:(h + 1) * HD, h * HD:(h + 1) * HD] for h in range(HEADS)], axis=1)


def _s5_block_diag_in(b):
    eye = jnp.eye(S5_GROUPS, dtype=F32)[:, None, :, None]
    return (jnp.transpose(b, (0, 2, 1))[:, :, None, :] * eye).reshape(S5_GROUPS * S5_CH, S5_N)


def _s5_block_diag_out(cm):
    eye = jnp.eye(S5_GROUPS, dtype=F32)[:, None, :, None]
    return (jnp.transpose(cm, (0, 2, 1))[:, :, None, :] * eye).reshape(S5_N, S5_GROUPS * S5_CH)


def kernel(x_prompt, x_sample, cache_sb_k, cache_sb_v, state_gdn_s, state_gdn_conv, state_mlstm_c, state_mlstm_n, state_mlstm_m, state_s5_re, state_s5_im, ffn1_norm, ffn1_w_gate, ffn1_w_up, ffn1_w_down, mix_norm, w_in, gdn_conv_w, gdn_a_log, gdn_dt_bias, gdn_norm, ml_b_i, ml_b_f, ml_norm, s5_a_re, s5_a_im, s5_b_re, s5_b_im, s5_c_re, s5_c_im, s5_d, s5_log_step, s5_w_glu, s5_b_glu, s5_norm, sb_norm, w_out, ffn2_norm, ffn2_w_gate, ffn2_w_up, ffn2_w_down, final_norm):
    depth = w_in.shape[0]
    n_p, len_p, d_model = x_prompt.shape
    n_s, len_s, _ = x_sample.shape
    past = cache_sb_k.shape[2]
    assert n_p == 1
    t_p, t_s = n_p * len_p, n_s * len_s
    c_p, c_s = min(64, len_p), min(64, len_s)
    g_p = 4 if len_p % (4 * c_p) == 0 else 1

    x = jnp.concatenate([x_prompt.reshape(t_p, d_model), x_sample.reshape(t_s, d_model)], axis=0)
    perm = _w_in_permutation(3 * WG + 2 * HEADS + WG, 3 * WG + 2 * HEADS + WG, WG)
    k_cache = cache_sb_k.reshape(depth, n_s, past, WG)
    v_cache = cache_sb_v.reshape(depth, n_s, past, WG)
    zeros = lambda *s: jnp.zeros(s, F32)
    row = lambda v: v.reshape(1, -1).astype(F32)
    outs_p = [[] for _ in range(9)]
    outs_s = [[] for _ in range(9)]

    for l in range(depth):
        x = _ffn(x, row(ffn1_norm[l]), ffn1_w_gate[l].astype(BF16), ffn1_w_up[l].astype(BF16),
                 ffn1_w_down[l].astype(BF16))
        w_p = jnp.where(perm[None, :] >= 0, w_in[l][:, np.maximum(perm, 0)], 0.0).astype(BF16)
        z, zd = _proj(x, row(mix_norm[l]), w_p)

        cw = jnp.concatenate([gdn_conv_w[l], zeros(8 - CONV_W, 3 * WG)], axis=0)
        alog, dtb = _lane_row(gdn_a_log[l], GDN_LANE), _lane_row(gdn_dt_bias[l], GDN_LANE)
        gn_a = row(jnp.tile(gdn_norm[l], HEADS))
        conv_s = jnp.concatenate([zeros(n_s, 8 - (CONV_W - 1), 3 * WG), state_gdn_conv[l]], axis=1)
        oa_p, sa_p = _gdn(z, 0, n_p, len_p, c_p, g_p, zeros(n_p, 8, 3 * WG), zeros(n_p, WG, WG), cw, alog, dtb, gn_a)
        oa_s, sa_s = _gdn(z, t_p, n_s, len_s, c_s, 1, conv_s, _to_block_diag(state_gdn_s[l]), cw, alog, dtb, gn_a)

        b_i, b_f = _lane_row(ml_b_i[l], ML_LANE), _lane_row(ml_b_f[l], ML_LANE)
        gn_b = row(ml_norm[l])
        ob_p, cb_p, nb_p, mb_p = _mlstm(z, 0, n_p, len_p, c_p, g_p, zeros(n_p, WG, WG), zeros(n_p, 1, WG),
                                        zeros(n_p, 1, LANES), b_i, b_f, gn_b)
        m0_s = jnp.zeros((n_s, 1, LANES), F32).at[:, 0, ML_LANE:ML_LANE + HEADS].set(state_mlstm_m[l])
        ob_s, cb_s, nb_s, mb_s = _mlstm(z, t_p, n_s, len_s, c_s, 1, _to_block_diag(state_mlstm_c[l]),
                                        state_mlstm_n[l].reshape(n_s, 1, WG), m0_s, b_i, b_f, gn_b)

        s5_consts = (row(s5_a_re[l]), row(s5_a_im[l]),
                     row(jnp.broadcast_to(jnp.exp(s5_log_step[l])[:, None], (S5_GROUPS, S5_P))),
                     _s5_block_diag_in(s5_b_re[l]).astype(BF16), _s5_block_diag_in(s5_b_im[l]).astype(BF16),
                     _s5_block_diag_out(s5_c_re[l]).astype(BF16), _s5_block_diag_out(s5_c_im[l]).astype(BF16),
                     row(s5_d[l]), s5_w_glu[l].astype(BF16), row(s5_b_glu[l]), row(s5_norm[l]))
        oc_p, xr_p, xi_p = _s5(z, 0, n_p, len_p, min(128, len_p), zeros(n_p, 1, S5_N), zeros(n_p, 1, S5_N),
                               *s5_consts)
        oc_s, xr_s, xi_s = _s5(z, t_p, n_s, len_s, len_s, state_s5_re[l].reshape(n_s, 1, S5_N),
                               state_s5_im[l].reshape(n_s, 1, S5_N), *s5_consts)

        gn_d = row(sb_norm[l])
        od_p = _sb_prompt(zd, len_p, gn_d)
        od_s = _sb_sample(zd, t_p, n_s, len_s, k_cache, v_cache, l, gn_d)

        mix = tuple(jnp.concatenate([p, s], axis=0) for p, s in
                    ((oa_p, oa_s), (ob_p, ob_s), (oc_p, oc_s), (od_p, od_s))) + (w_out[l].astype(BF16),)
        x = _ffn(x, row(ffn2_norm[l]), ffn2_w_gate[l].astype(BF16), ffn2_w_up[l].astype(BF16),
                 ffn2_w_down[l].astype(BF16), mix=mix, final_g=row(final_norm) if l == depth - 1 else None)

        k_new = z[:, Z_D_QKV + WG:Z_D_QKV + 2 * WG]
        v_new = z[:, Z_D_QKV + 2 * WG:Z_D_QKV + 3 * WG]
        z_conv = z[:, Z_A_QKV:Z_A_QKV + 3 * WG]
        tail = CONV_W - 1
        for outs, r0, n, ln, sa, cb, nb, mb, xr, xi in (
                (outs_p, 0, n_p, len_p, sa_p, cb_p, nb_p, mb_p, xr_p, xi_p),
                (outs_s, t_p, n_s, len_s, sa_s, cb_s, nb_s, mb_s, xr_s, xi_s)):
            rows = slice(r0, r0 + n * ln)
            outs[0].append(k_new[rows].reshape(n, ln, HEADS, HD))
            outs[1].append(v_new[rows].reshape(n, ln, HEADS, HD))
            outs[2].append(_from_block_diag(sa))
            outs[3].append(z_conv[rows].reshape(n, ln, 3 * WG)[:, ln - tail:, :])
            outs[4].append(_from_block_diag(cb))
            outs[5].append(nb.reshape(n, HEADS, HD))
            outs[6].append(mb[:, 0, ML_LANE:ML_LANE + HEADS])
            outs[7].append(xr.reshape(n, S5_GROUPS, S5_P))
            outs[8].append(xi.reshape(n, S5_GROUPS, S5_P))

    y_prompt = x[:t_p].reshape(n_p, len_p, d_model)
    y_sample = x[t_p:].reshape(n_s, len_s, d_model)
    return (y_prompt, y_sample, *[jnp.stack(o) for o in outs_p], *[jnp.stack(o) for o in outs_s])
```
